```python
import jax, jax.numpy as jnp
from jax import lax
import numpy as np

D_MODEL = 1024
BATCH = 8
SEQ = 2048
DEPTH = 4
DEC_BATCH = 128
DEC_SEQ = 1
PAST_LEN = 16384
PAGE_SIZE = 128

WA = D_MODEL
WB = D_MODEL
KCONV = 31
CHUNK = 128
N_GROUPS_B = 8
HD_B = WB // N_GROUPS_B
PLE_DIM = 256
EPS = 1e-6
SPLITS = (2 * WA, WA, WB, WB, WB, D_MODEL, D_MODEL)
N_IN = sum(SPLITS)

kernel_name = "hybrid_conv_gmlp_gated_decoder_step"


def rms_norm(x, g):
    xf = x.astype(jnp.float32)
    y = xf * lax.rsqrt(jnp.mean(xf * xf, axis=-1, keepdims=True) + EPS)
    return (y * g.astype(jnp.float32)).astype(x.dtype)


def layer_norm(x, g, b):
    xf = x.astype(jnp.float32)
    mu = jnp.mean(xf, axis=-1, keepdims=True)
    var = jnp.mean(jnp.square(xf - mu), axis=-1, keepdims=True)
    y = (xf - mu) * lax.rsqrt(var + EPS)
    return (y * g.astype(jnp.float32) + b.astype(jnp.float32)).astype(x.dtype)


def causal_depthwise_conv(buf, xa, w, b):
    xx = jnp.concatenate([buf.astype(xa.dtype), xa], axis=1)
    y = lax.conv_general_dilated(
        xx, w[:, None, :].astype(xa.dtype), window_strides=(1,), padding='VALID',
        dimension_numbers=('NWC', 'WIO', 'NWC'), feature_group_count=WA)
    return y + b, xx[:, -(KCONV - 1):]


def chunk_spatial_mix(v, w_s, b_s):
    n, t, _ = v.shape
    n_ch = -(-t // CHUNK)
    tp = n_ch * CHUNK
    vp = jnp.pad(v, ((0, 0), (0, tp - t), (0, 0)))
    vr = vp.reshape(n, n_ch, CHUNK, N_GROUPS_B, HD_B)
    mask = jnp.tril(jnp.ones((CHUNK, CHUNK), dtype=bool))
    wm = jnp.where(mask[None], w_s, jnp.zeros_like(w_s))
    s = jnp.einsum('gts,ncsgd->nctgd', wm, vr) + b_s.T[:, :, None]
    return s.reshape(n, tp, WB)[:, :t]


def trunk_layer(x, p_i, buf, norm_g, w_in, conv_w, conv_b, ln_a_g, ln_a_b, w_proj_a,
                ln_v_g, ln_v_b, w_spatial, b_spatial, w_proj_b, w_out,
                ple_norm_g, w_ple_gate, b_ple_gate, w_ple):
    h = rms_norm(x, norm_g)
    z = h @ w_in
    idx = np.cumsum(SPLITS)[:-1].tolist()
    glu_in, za, u, v, zb, ga, gb = jnp.split(z, idx, axis=-1)
    a_lin, a_gate = jnp.split(glu_in, 2, axis=-1)
    xa = a_lin * jax.nn.sigmoid(a_gate)
    ya, new_buf = causal_depthwise_conv(buf, xa, conv_w, conv_b)
    ya = jax.nn.silu(layer_norm(ya, ln_a_g, ln_a_b)) * jax.nn.silu(za)
    ya = ya @ w_proj_a
    vn = layer_norm(v, ln_v_g, ln_v_b)
    s = chunk_spatial_mix(vn, w_spatial, b_spatial)
    yb = (u * s * jax.nn.silu(zb)) @ w_proj_b
    m = jax.nn.sigmoid(ga) * ya + jax.nn.sigmoid(gb) * yb
    x = x + m @ w_out
    gp = jax.nn.sigmoid(rms_norm(x, ple_norm_g) @ w_ple_gate + b_ple_gate)
    x = x + gp * (p_i @ w_ple)
    return x, new_buf, vn


def setup_inputs(seed: int = 0) -> dict:
    key = jax.random.key(seed)
    ks = jax.random.split(key, 24)
    f32 = jnp.float32

    def nrm(k, shape, scale):
        return jax.random.normal(k, shape, f32) * scale

    return {
        "x_prompt": nrm(ks[0], (BATCH, SEQ, D_MODEL), 1.0),
        "x_sample": nrm(ks[1], (DEC_BATCH, DEC_SEQ, D_MODEL), 1.0),
        "state_conv": nrm(ks[2], (DEPTH, DEC_BATCH, KCONV - 1, WA), 0.5),
        "p_prompt": nrm(ks[3], (DEPTH, BATCH, SEQ, PLE_DIM), 1.0),
        "p_sample": nrm(ks[4], (DEPTH, DEC_BATCH, DEC_SEQ, PLE_DIM), 1.0),
        "norm_g": 1.0 + nrm(ks[5], (DEPTH, D_MODEL), 0.02),
        "w_in": nrm(ks[6], (DEPTH, D_MODEL, N_IN), D_MODEL ** -0.5),
        "conv_w": nrm(ks[7], (DEPTH, KCONV, WA), KCONV ** -0.5),
        "conv_b": nrm(ks[8], (DEPTH, WA), 0.02),
        "ln_a_g": 1.0 + nrm(ks[9], (DEPTH, WA), 0.02),
        "ln_a_b": nrm(ks[10], (DEPTH, WA), 0.02),
        "w_proj_a": nrm(ks[11], (DEPTH, WA, D_MODEL), WA ** -0.5),
        "ln_v_g": 1.0 + nrm(ks[12], (DEPTH, WB), 0.02),
        "ln_v_b": nrm(ks[13], (DEPTH, WB), 0.02),
        "w_spatial": nrm(ks[14], (DEPTH, N_GROUPS_B, CHUNK, CHUNK), CHUNK ** -0.5),
        "b_spatial": 1.0 + nrm(ks[15], (DEPTH, N_GROUPS_B, CHUNK), 0.02),
        "w_proj_b": nrm(ks[16], (DEPTH, WB, D_MODEL), WB ** -0.5),
        "w_out": nrm(ks[17], (DEPTH, D_MODEL, D_MODEL), D_MODEL ** -0.5),
        "ple_norm_g": 1.0 + nrm(ks[18], (DEPTH, D_MODEL), 0.02),
        "w_ple_gate": nrm(ks[19], (DEPTH, D_MODEL, D_MODEL), D_MODEL ** -0.5),
        "b_ple_gate": nrm(ks[20], (DEPTH, D_MODEL), 0.02),
        "w_ple": nrm(ks[21], (DEPTH, PLE_DIM, D_MODEL), PLE_DIM ** -0.5),
        "final_g": 1.0 + nrm(ks[22], (D_MODEL,), 0.02),
    }


def reference(x_prompt, x_sample, state_conv, p_prompt, p_sample, norm_g, w_in, conv_w,
              conv_b, ln_a_g, ln_a_b, w_proj_a, ln_v_g, ln_v_b, w_spatial, b_spatial,
              w_proj_b, w_out, ple_norm_g, w_ple_gate, b_ple_gate, w_ple, final_g):
    xp, xs = x_prompt, x_sample
    conv_p, conv_s, vrow_p, vrow_s = [], [], [], []
    zero_buf = jnp.zeros((xp.shape[0], KCONV - 1, WA), xp.dtype)
    for i in range(DEPTH):
        prm = (norm_g[i], w_in[i], conv_w[i], conv_b[i], ln_a_g[i], ln_a_b[i], w_proj_a[i],
               ln_v_g[i], ln_v_b[i], w_spatial[i], b_spatial[i], w_proj_b[i], w_out[i],
               ple_norm_g[i], w_ple_gate[i], b_ple_gate[i], w_ple[i])
        xp, bp, vp = trunk_layer(xp, p_prompt[i], zero_buf, *prm)
        xs, bs, vs = trunk_layer(xs, p_sample[i], state_conv[i], *prm)
        conv_p.append(bp)
        conv_s.append(bs)
        vrow_p.append(vp[:, -CHUNK:])
        vrow_s.append(vs)
    y_prompt = rms_norm(xp, final_g)
    y_sample = rms_norm(xs, final_g)
    return (y_prompt, y_sample, jnp.stack(conv_p), jnp.stack(conv_s),
            jnp.stack(vrow_p), jnp.stack(vrow_s))
```

```python
import functools

import jax
import jax.numpy as jnp
from jax.experimental import pallas as pl
from jax.experimental.pallas import tpu as pltpu

D = 1024
KCONV = 31
HIST = KCONV - 1
CHUNK = 128
GROUPS = 8
HD = D // GROUPS
PLE = 256
EPS = 1e-6
C_GLU, C_ZA, C_U, C_V, C_ZB, C_GA, C_GB, C_END = 0, 2048, 3072, 4096, 5120, 6144, 7168, 8192

TM = 256
CARRY = 32
SB = 32
VMEM_LIMIT = 56 * 1024 * 1024

V_NORM_G, V_CONV_B, V_LNA_G, V_LNA_B, V_LNV_G, V_LNV_B, V_PLE_G, V_PLE_B, V_FINAL_G = range(9)
N_VEC = 16


def _bf(x):
    return x.astype(jnp.bfloat16)


def _dot(a, b):
    return jnp.dot(a, b, preferred_element_type=jnp.float32)


def _rms(x, g):
    return x * jax.lax.rsqrt(jnp.mean(x * x, axis=-1, keepdims=True) + EPS) * g


def _ln(x, g, b):
    mu = jnp.mean(x, axis=-1, keepdims=True)
    xc = x - mu
    var = jnp.mean(xc * xc, axis=-1, keepdims=True)
    return xc * jax.lax.rsqrt(var + EPS) * g + b


def _sigmoid(x):
    return jax.nn.sigmoid(x)


def _silu(x):
    return x * jax.nn.sigmoid(x)


def _tail(x, h, ya_conv, s_of_vn, p, vec, w_in, w_pa, w_pb, w_out, w_pg, w_ple, final):
    za = _dot(h, w_in[:, C_ZA:C_U])
    ya = _silu(_ln(ya_conv, vec[V_LNA_G:V_LNA_G + 1], vec[V_LNA_B:V_LNA_B + 1])) * _silu(za)
    ya = _dot(_bf(ya), w_pa[...])
    v = _dot(h, w_in[:, C_V:C_ZB])
    vn = _ln(v, vec[V_LNV_G:V_LNV_G + 1], vec[V_LNV_B:V_LNV_B + 1])
    s = s_of_vn(vn)
    u = _dot(h, w_in[:, C_U:C_V])
    zb = _dot(h, w_in[:, C_ZB:C_GA])
    yb = _dot(_bf(u * s * _silu(zb)), w_pb[...])
    ga = _dot(h, w_in[:, C_GA:C_GB])
    gb = _dot(h, w_in[:, C_GB:C_END])
    m = _sigmoid(ga) * ya + _sigmoid(gb) * yb
    x = x + _dot(_bf(m), w_out[...])
    gate = _dot(_bf(_rms(x, vec[V_PLE_G:V_PLE_G + 1])), w_pg[...]) + vec[V_PLE_B:V_PLE_B + 1]
    x = x + _sigmoid(gate) * _dot(_bf(p), w_ple[...])
    if final:
        x = _rms(x, vec[V_FINAL_G:V_FINAL_G + 1])
    return x, vn


def _prompt_kernel(x_ref, p_ref, vec_ref, cw_ref, ws_ref, bs_ref, w_in, w_pa, w_pb, w_out, w_pg,
                   w_ple, xo_ref, conv_ref, vrow_ref, xx_ref, *, final):
    j = pl.program_id(1)
    nj = pl.num_programs(1)
    vec = vec_ref[...]
    x = x_ref[0]
    h = _bf(_rms(x, vec[V_NORM_G:V_NORM_G + 1]))

    glu = _dot(h, w_in[:, C_GLU:C_ZA])
    xa = glu[:, :D] * _sigmoid(glu[:, D:])

    @pl.when(j == 0)
    def _():
        xx_ref[0:CARRY, :] = jnp.zeros((CARRY, D), jnp.float32)

    xx_ref[CARRY:CARRY + TM, :] = xa
    acc = jnp.zeros((TM, D), jnp.float32) + vec[V_CONV_B:V_CONV_B + 1]
    for k in range(KCONV):
        off = CARRY - HIST + k
        acc = acc + xx_ref[off:off + TM, :] * cw_ref[k:k + 1, :]
    xx_ref[0:CARRY, :] = xx_ref[TM:TM + CARRY, :]

    row = jax.lax.broadcasted_iota(jnp.int32, (CHUNK, CHUNK), 0)
    col = jax.lax.broadcasted_iota(jnp.int32, (CHUNK, CHUNK), 1)
    causal = row >= col
    bs = bs_ref[...]

    def s_of_vn(vn):
        vb = _bf(vn)
        cols = []
        for g in range(GROUPS):
            wm = _bf(jnp.where(causal, ws_ref[g], 0.0))
            bias = bs[:, g:g + 1]
            rows = [_dot(wm, vb[c * CHUNK:(c + 1) * CHUNK, g * HD:(g + 1) * HD]) + bias
                    for c in range(TM // CHUNK)]
            cols.append(jnp.concatenate(rows, axis=0))
        return jnp.concatenate(cols, axis=1)

    xo, vn = _tail(x, h, acc, s_of_vn, p_ref[0], vec, w_in, w_pa, w_pb, w_out, w_pg, w_ple, final)
    xo_ref[0] = xo

    @pl.when(j == nj - 1)
    def _():
        conv_ref[0] = xx_ref[CARRY + TM - HIST:CARRY + TM, :]
        vrow_ref[0] = vn[TM - CHUNK:, :]


def _sample_kernel(x_ref, p_ref, st_ref, vec_ref, cw_ref, ws_ref, bs_ref, w_in, w_pa, w_pb, w_out,
                   w_pg, w_ple, xo_ref, conv_ref, vrow_ref, xa_ref, h_ref, acc_ref, *, final, nb):
    j = pl.program_id(0)
    vec = vec_ref[...]

    @pl.when(j == 0)
    def _():
        h = _bf(_rms(x_ref[...], vec[V_NORM_G:V_NORM_G + 1]))
        h_ref[...] = h
        glu = _dot(h, w_in[:, C_GLU:C_ZA])
        xa_ref[...] = glu[:, :D] * _sigmoid(glu[:, D:])

    st = st_ref[...]
    r0 = pl.multiple_of(j * SB, SB)
    xa_blk = xa_ref[pl.ds(r0, SB), :]
    conv_ref[:, 0:HIST - 1, :] = st[:, 1:HIST, :]
    conv_ref[:, HIST - 1:HIST, :] = xa_blk[:, None, :]
    cw = cw_ref[...]
    part = jnp.sum(st * cw[None, 0:HIST, :], axis=1)
    acc_ref[pl.ds(r0, SB), :] = part + xa_blk * cw[HIST:KCONV, :] + vec[V_CONV_B:V_CONV_B + 1]

    @pl.when(j == nb - 1)
    def _():
        lane_g = jax.lax.broadcasted_iota(jnp.int32, (1, D), 1) // HD
        w00 = jnp.zeros((1, D), jnp.float32)
        b0 = jnp.zeros((1, D), jnp.float32)
        for g in range(GROUPS):
            w00 = jnp.where(lane_g == g, ws_ref[g, 0:1, 0:1], w00)
            b0 = jnp.where(lane_g == g, bs_ref[0:1, g:g + 1], b0)

        def s_of_vn(vn):
            return _bf(w00).astype(jnp.float32) * _bf(vn).astype(jnp.float32) + b0

        xo, vn = _tail(x_ref[...], h_ref[...], acc_ref[...], s_of_vn, p_ref[...], vec, w_in, w_pa,
                       w_pb, w_out, w_pg, w_ple, final)
        xo_ref[...] = xo
        vrow_ref[...] = vn


def _resident(shape):
    nd = len(shape)
    return pl.BlockSpec(shape, lambda *_: (0,) * nd, pipeline_mode=pl.Buffered(1))


def _weight_specs():
    return [_resident((D, C_END)), _resident((D, D)), _resident((D, D)), _resident((D, D)),
            _resident((D, D)), _resident((PLE, D))]


def _prompt_layer(x, p, vec, cw, ws, bs, weights, final):
    b, t, _ = x.shape
    assert t % TM == 0 and TM % CHUNK == 0 and t >= HIST
    grid = (b, t // TM)
    in_specs = [
        pl.BlockSpec((1, TM, D), lambda i, j: (i, j, 0)),
        pl.BlockSpec((1, TM, PLE), lambda i, j: (i, j, 0)),
        _resident((N_VEC, D)), _resident((CARRY, D)), _resident((GROUPS, CHUNK, CHUNK)),
        _resident((CHUNK, GROUPS)),
    ] + _weight_specs()
    out_specs = [
        pl.BlockSpec((1, TM, D), lambda i, j: (i, j, 0)),
        pl.BlockSpec((1, HIST, D), lambda i, j: (i, 0, 0)),
        pl.BlockSpec((1, CHUNK, D), lambda i, j: (i, 0, 0)),
    ]
    out_shape = [
        jax.ShapeDtypeStruct((b, t, D), jnp.float32),
        jax.ShapeDtypeStruct((b, HIST, D), jnp.float32),
        jax.ShapeDtypeStruct((b, CHUNK, D), jnp.float32),
    ]
    return pl.pallas_call(
        functools.partial(_prompt_kernel, final=final),
        grid=grid, in_specs=in_specs, out_specs=out_specs, out_shape=out_shape,
        scratch_shapes=[pltpu.VMEM((CARRY + TM, D), jnp.float32)],
        compiler_params=pltpu.CompilerParams(
            dimension_semantics=("arbitrary", "arbitrary"), vmem_limit_bytes=VMEM_LIMIT),
        name="prompt_layer",
    )(x, p, vec, cw, ws, bs, *weights)


def _sample_layer(x, p, st, vec, cw, ws, bs, weights, final):
    n = x.shape[0]
    assert n % SB == 0
    nb = n // SB
    in_specs = [
        _resident((n, D)), _resident((n, PLE)),
        pl.BlockSpec((SB, HIST, D), lambda j: (j, 0, 0)),
        _resident((N_VEC, D)), _resident((CARRY, D)), _resident((GROUPS, CHUNK, CHUNK)),
        _resident((CHUNK, GROUPS)),
    ] + _weight_specs()
    out_specs = [
        pl.BlockSpec((n, D), lambda j: (0, 0)),
        pl.BlockSpec((SB, HIST, D), lambda j: (j, 0, 0)),
        pl.BlockSpec((n, D), lambda j: (0, 0)),
    ]
    out_shape = [
        jax.ShapeDtypeStruct((n, D), jnp.float32),
        jax.ShapeDtypeStruct((n, HIST, D), jnp.float32),
        jax.ShapeDtypeStruct((n, D), jnp.float32),
    ]
    return pl.pallas_call(
        functools.partial(_sample_kernel, final=final, nb=nb),
        grid=(nb,), in_specs=in_specs, out_specs=out_specs, out_shape=out_shape,
        scratch_shapes=[pltpu.VMEM((n, D), jnp.float32), pltpu.VMEM((n, D), jnp.bfloat16),
                        pltpu.VMEM((n, D), jnp.float32)],
        compiler_params=pltpu.CompilerParams(
            dimension_semantics=("arbitrary",), vmem_limit_bytes=VMEM_LIMIT),
        name="sample_layer",
    )(x, p, st, vec, cw, ws, bs, *weights)


def kernel(x_prompt, x_sample, state_conv, p_prompt, p_sample, norm_g, w_in, conv_w, conv_b, ln_a_g, ln_a_b, w_proj_a, ln_v_g, ln_v_b, w_spatial, b_spatial, w_proj_b, w_out, ple_norm_g, w_ple_gate, b_ple_gate, w_ple, final_g):
    depth = w_in.shape[0]
    n_s = x_sample.shape[0]
    assert x_sample.shape[1] == 1
    xp = x_prompt
    xs = x_sample.reshape(n_s, D)
    ps = p_sample.reshape(depth, n_s, PLE)
    final_rows = jnp.broadcast_to(final_g[None, :], (depth, D))
    pad_rows = jnp.zeros((depth, N_VEC - 9, D), jnp.float32)
    vecs = jnp.concatenate(
        [jnp.stack([norm_g, conv_b, ln_a_g, ln_a_b, ln_v_g, ln_v_b, ple_norm_g, b_ple_gate,
                    final_rows], axis=1), pad_rows], axis=1)
    cws = jnp.pad(conv_w, ((0, 0), (0, CARRY - KCONV), (0, 0)))
    bst = jnp.swapaxes(b_spatial, 1, 2)
    conv_p, conv_s, vrow_p, vrow_s = [], [], [], []
    for i in range(depth):
        weights = tuple(_bf(w[i]) for w in (w_in, w_proj_a, w_proj_b, w_out, w_ple_gate, w_ple))
        final = i == depth - 1
        xp, cp, vp = _prompt_layer(xp, p_prompt[i], vecs[i], cws[i], w_spatial[i], bst[i],
                                   weights, final)
        xs, cs, vs = _sample_layer(xs, ps[i], state_conv[i], vecs[i], cws[i], w_spatial[i],
                                   bst[i], weights, final)
        conv_p.append(cp)
        conv_s.append(cs)
        vrow_p.append(vp)
        vrow_s.append(vs)
    return (xp, xs.reshape(n_s, 1, D), jnp.stack(conv_p), jnp.stack(conv_s),
            jnp.stack(vrow_p), jnp.stack(vrow_s).reshape(depth, n_s, 1, D))
```

```python
import functools
import itertools

import jax
import jax.numpy as jnp
from jax.experimental import pallas as pl
from jax.experimental.pallas import tpu as pltpu

D = 1024
KCONV = 31
HIST = KCONV - 1
CHUNK = 128
GROUPS = 8
HD = D // GROUPS
PLE = 256
EPS = 1e-6
C_GLU, C_ZA, C_U, C_V, C_ZB, C_GA, C_GB, C_END = 0, 2048, 3072, 4096, 5120, 6144, 7168, 8192

TM = 256
CARRY = 32
SUB = 8
LEAD = CARRY - HIST
LANE = 128
MXU_N = 256
RB = 32
SB = 32
VMEM_LIMIT = 56 * 1024 * 1024

V_NORM_G, V_CONV_B, V_LNA_G, V_LNA_B, V_LNV_G, V_LNV_B, V_PLE_G, V_PLE_B, V_FINAL_G = range(9)
N_VEC = 16


def _bf(x):
    return x.astype(jnp.bfloat16)


def _dot(a, b):
    return jnp.dot(a, b, preferred_element_type=jnp.float32)


def _w(ref, c0=None, c1=None):
    words = ref[...] if c0 is None else ref[:, c0:c1]
    return pltpu.bitcast(words, jnp.bfloat16)


def _rms(x, g):
    return x * jax.lax.rsqrt(jnp.mean(x * x, axis=-1, keepdims=True) + EPS) * g


def _ln(x, g, b):
    mu = jnp.mean(x, axis=-1, keepdims=True)
    xc = x - mu
    var = jnp.mean(xc * xc, axis=-1, keepdims=True)
    return xc * jax.lax.rsqrt(var + EPS) * g + b


def _sigmoid(x):
    return jax.nn.sigmoid(x)


def _silu(x):
    return x * jax.nn.sigmoid(x)


def _row(vec, i):
    return vec[i:i + 1]


def _zero_row_of(v):
    bits = jax.lax.bitcast_convert_type(v, jnp.uint32)
    zero = jax.lax.shift_right_logical(jax.lax.shift_right_logical(bits, jnp.uint32(16)),
                                       jnp.uint32(16))
    return jax.lax.bitcast_convert_type(zero, jnp.float32)[0:1, :]


def _conv_lane_tile(xx_ref, cw_ref, acc_ref, bias, l0, take_zero):
    lanes = slice(l0, l0 + LANE)
    w = [cw_ref[k:k + 1, lanes] for k in range(KCONV)]

    def partial(r, row0, rows):
        part = None
        for q in range((KCONV + LEAD + SUB - 1) // SUB):
            k = SUB * q + r - LEAD
            if 0 <= k < KCONV:
                term = xx_ref[row0 + SUB * q:row0 + SUB * q + rows, lanes] * w[k]
                part = term if part is None else part + term
        return part

    tails = {r: partial(r, 0, SUB) for r in range(1, SUB)}
    for r0 in range(0, TM, RB):
        acc = partial(0, r0, RB) + bias[:, lanes]
        zero = take_zero()
        if zero is not None:
            acc = acc + zero
        for r in range(1, SUB):
            new = partial(r, r0 + SUB, RB)
            window = jnp.concatenate([tails[r], new], axis=0)
            acc = acc + window[r:r + RB]
            tails[r] = new[RB - SUB:]
        acc_ref[r0:r0 + RB, lanes] = acc
        yield
    xx_ref[0:CARRY, lanes] = xx_ref[TM:TM + CARRY, lanes]
    yield


def _tail(x, z, ya_conv, s_of_vn, p, vec, w_pa, w_pb, w_out, w_pg, w_ple, final):
    ya = _silu(_ln(ya_conv, _row(vec, V_LNA_G), _row(vec, V_LNA_B))) * _silu(z(C_ZA, C_U))
    ya = _dot(_bf(ya), _w(w_pa))
    vn = _ln(z(C_V, C_ZB), _row(vec, V_LNV_G), _row(vec, V_LNV_B))
    yb = _dot(_bf(z(C_U, C_V) * s_of_vn(vn) * _silu(z(C_ZB, C_GA))), _w(w_pb))
    m = _sigmoid(z(C_GA, C_GB)) * ya + _sigmoid(z(C_GB, C_END)) * yb
    x = x + _dot(_bf(m), _w(w_out))
    gate = _dot(_bf(_rms(x, _row(vec, V_PLE_G))), _w(w_pg)) + _row(vec, V_PLE_B)
    x = x + _sigmoid(gate) * _dot(_bf(p), _w(w_ple))
    if final:
        x = _rms(x, _row(vec, V_FINAL_G))
    return x, vn


def _prompt_kernel(x_ref, p_ref, vec_ref, cw_ref, ws_ref, bs_ref, w_in, w_pa, w_pb, w_out, w_pg,
                   w_ple, xo_ref, conv_ref, vrow_ref, xx_ref, z_ref, acc_ref, *, final):
    j = pl.program_id(1)

    @pl.when(j == 0)
    def _():
        xx_ref[0:CARRY, :] = jnp.zeros((CARRY, D), jnp.float32)

    vec = vec_ref[...]
    x = x_ref[...]
    h = _bf(_rms(x, _row(vec, V_NORM_G)))

    bias = _row(vec, V_CONV_B)
    popped = []
    in_flight = []

    def proj_piece(c0):
        res = _dot(h, _w(w_in, c0, c0 + MXU_N))
        z_ref[:, c0:c0 + MXU_N] = res
        popped.extend(in_flight)
        in_flight[:] = [_zero_row_of(res[0:SUB, 0:LANE])]

    def take_zero():
        zero = None
        while popped:
            row = popped.pop()
            zero = row if zero is None else zero + row
        return zero

    def lane_tile_work(l0):
        lanes = slice(l0, l0 + LANE)
        gate = z_ref[:, D + l0:D + l0 + LANE]
        xx_ref[CARRY:CARRY + TM, lanes] = z_ref[:, lanes] * _sigmoid(gate)
        yield
        yield from _conv_lane_tile(xx_ref, cw_ref, acc_ref, bias, l0, take_zero)

    glu_cols = [c for c0 in range(0, D, MXU_N) for c in (c0, D + c0)]
    pieces = [functools.partial(proj_piece, c0)
              for c0 in glu_cols + list(range(C_ZA, C_END, MXU_N))]
    tiles = list(range(0, D, LANE))
    blocks_per_tile = TM // RB + 2
    n_blocks = len(tiles) * blocks_per_tile
    blocks = itertools.chain(*[lane_tile_work(l0) for l0 in tiles])
    issued = 0
    for n in range(n_blocks):
        l0 = tiles[n // blocks_per_tile]
        needed = 2 * (l0 // MXU_N) + 2
        while issued < len(pieces) and (issued < needed or issued * n_blocks <= n * len(pieces)):
            pieces[issued]()
            issued += 1
        next(blocks)
    assert issued == len(pieces)

    row = jax.lax.broadcasted_iota(jnp.int32, (CHUNK, CHUNK), 0)
    col = jax.lax.broadcasted_iota(jnp.int32, (CHUNK, CHUNK), 1)
    causal = row >= col
    bs = bs_ref[...]

    def s_of_vn(vn):
        vb = _bf(vn)
        cols = []
        for g in range(GROUPS):
            wm = _bf(jnp.where(causal, ws_ref[g], 0.0))
            bias = bs[:, g:g + 1]
            rows = [_dot(wm, vb[c * CHUNK:(c + 1) * CHUNK, g * HD:(g + 1) * HD]) + bias
                    for c in range(TM // CHUNK)]
            cols.append(jnp.concatenate(rows, axis=0))
        return jnp.concatenate(cols, axis=1)

    def z(c0, c1):
        return z_ref[:, c0:c1]

    conv_ref[...] = xx_ref[CARRY + TM - HIST:CARRY + TM, :]

    @pl.when(j >= 0)
    def _():
        xo, vn = _tail(x_ref[...], z, acc_ref[...], s_of_vn, p_ref[...], vec, w_pa, w_pb, w_out,
                       w_pg, w_ple, final)
        xo_ref[...] = xo
        vrow_ref[...] = vn[TM - CHUNK:, :]


def _sample_kernel(x_ref, p_ref, st_ref, vec_ref, cw_ref, ws_ref, bs_ref, w_in, w_pa, w_pb, w_out,
                   w_pg, w_ple, xo_ref, conv_ref, vrow_ref, xa_ref, h_ref, acc_ref, *, final, nb):
    j = pl.program_id(0)
    vec = vec_ref[...]

    @pl.when(j == 0)
    def _():
        h = _bf(_rms(x_ref[...], _row(vec, V_NORM_G)))
        h_ref[...] = h
        glu = _dot(h, _w(w_in, C_GLU, C_ZA))
        xa_ref[...] = glu[:, :D] * _sigmoid(glu[:, D:])

    st = st_ref[...]
    r0 = pl.multiple_of(j * SB, SB)
    xa_blk = xa_ref[pl.ds(r0, SB), :]
    conv_ref[:, 0:HIST - 1, :] = st[:, 1:HIST, :]
    conv_ref[:, HIST - 1:HIST, :] = xa_blk[:, None, :]
    cw = cw_ref[...]
    part = jnp.sum(st * cw[None, 0:HIST, :], axis=1)
    acc_ref[pl.ds(r0, SB), :] = part + xa_blk * cw[HIST:KCONV, :] + _row(vec, V_CONV_B)

    @pl.when(j == nb - 1)
    def _():
        lane_g = jax.lax.broadcasted_iota(jnp.int32, (1, D), 1) // HD
        w00 = jnp.zeros((1, D), jnp.float32)
        b0 = jnp.zeros((1, D), jnp.float32)
        for g in range(GROUPS):
            w00 = jnp.where(lane_g == g, ws_ref[g, 0:1, 0:1], w00)
            b0 = jnp.where(lane_g == g, bs_ref[0:1, g:g + 1], b0)

        def s_of_vn(vn):
            return _bf(w00).astype(jnp.float32) * _bf(vn).astype(jnp.float32) + b0

        h = h_ref[...]

        def z(c0, c1):
            return _dot(h, _w(w_in, c0, c1))

        xo, vn = _tail(x_ref[...], z, acc_ref[...], s_of_vn, p_ref[...], vec, w_pa, w_pb, w_out,
                       w_pg, w_ple, final)
        xo_ref[...] = xo
        vrow_ref[...] = vn


def _layer_block(layer, shape):
    nd = len(shape)
    return pl.BlockSpec((None,) + tuple(shape), lambda *_: (layer,) + (0,) * nd,
                        pipeline_mode=pl.Buffered(1))


def _shared_specs(layer):
    return [
        _layer_block(layer, (N_VEC, D)), _layer_block(layer, (CARRY, D)),
        _layer_block(layer, (GROUPS, CHUNK, CHUNK)), _layer_block(layer, (CHUNK, GROUPS)),
        _layer_block(layer, (D // 2, C_END)), _layer_block(layer, (D // 2, D)),
        _layer_block(layer, (D // 2, D)), _layer_block(layer, (D // 2, D)),
        _layer_block(layer, (D // 2, D)), _layer_block(layer, (PLE // 2, D)),
    ]


def _prompt_layer(layer, x, p, shared, final):
    b, t, _ = x.shape
    assert t % TM == 0 and TM % CHUNK == 0 and t >= HIST
    in_specs = [
        pl.BlockSpec((None, TM, D), lambda i, j: (i, j, 0)),
        pl.BlockSpec((None, None, TM, PLE), lambda i, j: (layer, i, j, 0)),
    ] + _shared_specs(layer)
    out_specs = [
        pl.BlockSpec((None, TM, D), lambda i, j: (i, j, 0)),
        pl.BlockSpec((None, HIST, D), lambda i, j: (i, 0, 0)),
        pl.BlockSpec((None, CHUNK, D), lambda i, j: (i, 0, 0)),
    ]
    out_shape = [
        jax.ShapeDtypeStruct((b, t, D), jnp.float32),
        jax.ShapeDtypeStruct((b, HIST, D), jnp.float32),
        jax.ShapeDtypeStruct((b, CHUNK, D), jnp.float32),
    ]
    return pl.pallas_call(
        functools.partial(_prompt_kernel, final=final),
        grid=(b, t // TM), in_specs=in_specs, out_specs=out_specs, out_shape=out_shape,
        scratch_shapes=[pltpu.VMEM((CARRY + TM, D), jnp.float32),
                        pltpu.VMEM((TM, C_END), jnp.float32),
                        pltpu.VMEM((TM, D), jnp.float32)],
        compiler_params=pltpu.CompilerParams(
            dimension_semantics=("arbitrary", "arbitrary"), vmem_limit_bytes=VMEM_LIMIT),
        name="prompt_layer",
    )(x, p, *shared)


def _sample_layer(layer, x, p, st, shared, final):
    n = x.shape[0]
    assert n % SB == 0
    nb = n // SB
    in_specs = [
        pl.BlockSpec((n, D), lambda j: (0, 0), pipeline_mode=pl.Buffered(1)),
        _layer_block(layer, (n, PLE)),
        pl.BlockSpec((None, SB, HIST, D), lambda j: (layer, j, 0, 0)),
    ] + _shared_specs(layer)
    out_specs = [
        pl.BlockSpec((n, D), lambda j: (0, 0)),
        pl.BlockSpec((SB, HIST, D), lambda j: (j, 0, 0)),
        pl.BlockSpec((n, D), lambda j: (0, 0)),
    ]
    out_shape = [
        jax.ShapeDtypeStruct((n, D), jnp.float32),
        jax.ShapeDtypeStruct((n, HIST, D), jnp.float32),
        jax.ShapeDtypeStruct((n, D), jnp.float32),
    ]
    return pl.pallas_call(
        functools.partial(_sample_kernel, final=final, nb=nb),
        grid=(nb,), in_specs=in_specs, out_specs=out_specs, out_shape=out_shape,
        scratch_shapes=[pltpu.VMEM((n, D), jnp.float32), pltpu.VMEM((n, D), jnp.bfloat16),
                        pltpu.VMEM((n, D), jnp.float32)],
        compiler_params=pltpu.CompilerParams(
            dimension_semantics=("arbitrary",), vmem_limit_bytes=VMEM_LIMIT),
        name="sample_layer",
    )(x, p, st, *shared)


def _pack_rows(w):
    depth, k, n = w.shape
    pairs = jnp.swapaxes(_bf(w).reshape(depth, k // 2, 2, n), 2, 3)
    return jax.lax.bitcast_convert_type(pairs, jnp.uint32)


def kernel(x_prompt, x_sample, state_conv, p_prompt, p_sample, norm_g, w_in, conv_w, conv_b, ln_a_g, ln_a_b, w_proj_a, ln_v_g, ln_v_b, w_spatial, b_spatial, w_proj_b, w_out, ple_norm_g, w_ple_gate, b_ple_gate, w_ple, final_g):
    depth = w_in.shape[0]
    n_s = x_sample.shape[0]
    assert x_sample.shape[1] == 1
    xp = x_prompt
    xs = x_sample.reshape(n_s, D)
    ps = p_sample.reshape(depth, n_s, PLE)
    final_rows = jnp.broadcast_to(final_g[None, :], (depth, D))
    pad_rows = jnp.zeros((depth, N_VEC - 9, D), jnp.float32)
    vecs = jnp.concatenate(
        [jnp.stack([norm_g, conv_b, ln_a_g, ln_a_b, ln_v_g, ln_v_b, ple_norm_g, b_ple_gate,
                    final_rows], axis=1), pad_rows], axis=1)
    cws = jnp.pad(conv_w, ((0, 0), (0, CARRY - KCONV), (0, 0)))
    bst = jnp.swapaxes(b_spatial, 1, 2)
    shared = (vecs, cws, w_spatial, bst) + tuple(
        _pack_rows(w) for w in (w_in, w_proj_a, w_proj_b, w_out, w_ple_gate, w_ple))
    conv_p, conv_s, vrow_p, vrow_s = [], [], [], []
    for i in range(depth):
        final = i == depth - 1
        xp, cp, vp = _prompt_layer(i, xp, p_prompt, shared, final)
        xs, cs, vs = _sample_layer(i, xs, ps, state_conv, shared, final)
        conv_p.append(cp)
        conv_s.append(cs)
        vrow_p.append(vp)
        vrow_s.append(vs)
    return (xp, xs.reshape(n_s, 1, D), jnp.stack(conv_p), jnp.stack(conv_s),
            jnp.stack(vrow_p), jnp.stack(vrow_s).reshape(depth, n_s, 1, D))
```

```python
import functools
import itertools

import jax
import jax.numpy as jnp
from jax.experimental import pallas as pl
from jax.experimental.pallas import tpu as pltpu

D = 1024
KCONV = 31
HIST = KCONV - 1
CHUNK = 128
GROUPS = 8
HD = D // GROUPS
PLE = 256
EPS = 1e-6
C_GLU, C_ZA, C_U, C_V, C_ZB, C_GA, C_GB, C_END = 0, 2048, 3072, 4096, 5120, 6144, 7168, 8192

TM = 256
CARRY = 32
SUB = 8
LEAD = CARRY - HIST
LANE = 128
MXU_N = 256
RB = 32
COST_GATE, COST_CONV = 256, 330
KS = 6
PACK_N = 2048
VMEM_LIMIT = 56 * 1024 * 1024

V_NORM_G, V_CONV_B, V_LNA_G, V_LNA_B, V_LNV_G, V_LNV_B, V_PLE_G, V_PLE_B, V_FINAL_G = range(9)
N_VEC = 16


def _bf(x):
    return x.astype(jnp.bfloat16)


def _dot(a, b):
    return jnp.dot(a, b, preferred_element_type=jnp.float32)


def _w(ref, c0=None, c1=None):
    words = ref[...] if c0 is None else ref[:, c0:c1]
    return pltpu.bitcast(words, jnp.bfloat16)


def _rms(x, g):
    return x * jax.lax.rsqrt(jnp.mean(x * x, axis=-1, keepdims=True) + EPS) * g


def _ln(x, g, b):
    mu = jnp.mean(x, axis=-1, keepdims=True)
    xc = x - mu
    var = jnp.mean(xc * xc, axis=-1, keepdims=True)
    return xc * jax.lax.rsqrt(var + EPS) * g + b


def _sigmoid(x):
    return jax.nn.sigmoid(x)


def _silu(x):
    return x * jax.nn.sigmoid(x)


def _row(vec, i):
    return vec[i:i + 1]


def _zero_row_of(v):
    bits = jax.lax.bitcast_convert_type(v, jnp.uint32)
    zero = jax.lax.shift_right_logical(jax.lax.shift_right_logical(bits, jnp.uint32(16)),
                                       jnp.uint32(16))
    return jax.lax.bitcast_convert_type(zero, jnp.float32)[0:1, :]


def _conv_lane_tile(xx_ref, cw_ref, acc_ref, bias, l0, take_zero):
    lanes = slice(l0, l0 + LANE)
    w = [cw_ref[k:k + 1, lanes] for k in range(KCONV)]

    def partial(r, row0, rows):
        part = None
        for q in range((KCONV + LEAD + SUB - 1) // SUB):
            k = SUB * q + r - LEAD
            if 0 <= k < KCONV:
                term = xx_ref[row0 + SUB * q:row0 + SUB * q + rows, lanes] * w[k]
                part = term if part is None else part + term
        return part

    tails = {r: partial(r, 0, SUB) for r in range(1, SUB)}
    for r0 in range(0, TM, RB):
        acc = partial(0, r0, RB) + bias[:, lanes]
        zero = take_zero()
        if zero is not None:
            acc = acc + zero
        for r in range(1, SUB):
            new = partial(r, r0 + SUB, RB)
            window = jnp.concatenate([tails[r], new], axis=0)
            acc = acc + pltpu.roll(window, RB + SUB - r, axis=0)[0:RB]
            tails[r] = new[RB - SUB:]
        acc_ref[r0:r0 + RB, lanes] = acc
        yield
    xx_ref[0:CARRY, lanes] = xx_ref[TM:TM + CARRY, lanes]
    yield


def _branch_inputs(conv, za, v, vec):
    ya = _silu(_ln(conv, _row(vec, V_LNA_G), _row(vec, V_LNA_B))) * _silu(za)
    return _bf(ya), _ln(v, _row(vec, V_LNV_G), _row(vec, V_LNV_B))


def _tail(x, z, ya_in, vn, s_of_vn, p, vec, w_pa, w_pb, w_out, w_pg, w_ple, final):
    ya = _dot(ya_in, _w(w_pa))
    yb = _dot(_bf(z(C_U, C_V) * s_of_vn(vn) * _silu(z(C_ZB, C_GA))), _w(w_pb))
    m = _sigmoid(z(C_GA, C_GB)) * ya + _sigmoid(z(C_GB, C_END)) * yb
    x = x + _dot(_bf(m), _w(w_out))
    gate = _dot(_bf(_rms(x, _row(vec, V_PLE_G))), _w(w_pg)) + _row(vec, V_PLE_B)
    x = x + _sigmoid(gate) * _dot(_bf(p), _w(w_ple))
    if final:
        x = _rms(x, _row(vec, V_FINAL_G))
    return x


def _prompt_kernel(x_ref, p_ref, vec_ref, cw_ref, ws_ref, bs_ref, w_in, w_pa, w_pb, w_out, w_pg,
                   w_ple, xo_ref, conv_ref, vrow_ref, xx_ref, z_ref, acc_ref, h_ref, *, final):
    j = pl.program_id(1)

    @pl.when(j == 0)
    def _():
        xx_ref[0:CARRY, :] = jnp.zeros((CARRY, D), jnp.float32)

    vec = vec_ref[...]
    h_ref[...] = _bf(_rms(x_ref[...], _row(vec, V_NORM_G)))

    bias = _row(vec, V_CONV_B)
    popped = []
    in_flight = []

    def proj_piece(c0):
        res = _dot(h_ref[...], _w(w_in, c0, c0 + MXU_N))
        z_ref[:, c0:c0 + MXU_N] = res
        popped.extend(in_flight)
        in_flight[:] = [_zero_row_of(res[0:SUB, 0:LANE])]

    def take_zero():
        zero = None
        while popped:
            row = popped.pop()
            zero = row if zero is None else zero + row
        return zero

    def interleave(pieces, blocks, total_cost, needed):
        issued = spent = n = 0
        while True:
            while issued < len(pieces) and (issued < needed(n)
                                            or issued * total_cost <= spent * len(pieces)):
                proj_piece(pieces[issued])
                issued += 1
            cost = next(blocks, None)
            if cost is None:
                break
            spent += cost
            n += 1
        assert spent == total_cost and issued == len(pieces)

    def lane_tile_work(l0):
        lanes = slice(l0, l0 + LANE)
        gate = z_ref[:, D + l0:D + l0 + LANE]
        xx_ref[CARRY:CARRY + TM, lanes] = z_ref[:, lanes] * _sigmoid(gate)
        yield COST_GATE
        for _ in _conv_lane_tile(xx_ref, cw_ref, acc_ref, bias, l0, take_zero):
            yield COST_CONV

    glu_cols = [c for c0 in range(0, D, MXU_N) for c in (c0, D + c0)]
    tiles = list(range(0, D, LANE))
    blocks_per_tile = TM // RB + 2
    interleave(glu_cols + list(range(C_ZA, C_END, MXU_N)),
               itertools.chain(*[lane_tile_work(l0) for l0 in tiles]),
               len(tiles) * (COST_GATE + (blocks_per_tile - 1) * COST_CONV),
               lambda n: 2 * (tiles[min(n // blocks_per_tile, len(tiles) - 1)] // MXU_N) + 2)

    conv_ref[...] = xx_ref[CARRY + TM - HIST:CARRY + TM, :]

    row = jax.lax.broadcasted_iota(jnp.int32, (CHUNK, CHUNK), 0)
    col = jax.lax.broadcasted_iota(jnp.int32, (CHUNK, CHUNK), 1)
    causal = row >= col
    bs = bs_ref[...]
    n_ch = TM // CHUNK

    def s_of_vn(vn):
        vb = _bf(vn)
        cols = []
        for g in range(GROUPS):
            wm = _bf(jnp.where(causal, ws_ref[g], 0.0))
            rhs = jnp.concatenate([vb[c * CHUNK:(c + 1) * CHUNK, g * HD:(g + 1) * HD]
                                   for c in range(n_ch)], axis=1)
            res = _dot(wm, rhs) + bs[:, g:g + 1]
            cols.append(jnp.concatenate([res[:, c * HD:(c + 1) * HD] for c in range(n_ch)],
                                        axis=0))
        return jnp.concatenate(cols, axis=1)

    def z(c0, c1):
        return z_ref[:, c0:c1]

    @pl.when(j >= 0)
    def _():
        ya_in, vn = _branch_inputs(acc_ref[...], z(C_ZA, C_U), z(C_V, C_ZB), vec)
        vrow_ref[...] = vn[TM - CHUNK:, :]
        xo_ref[...] = _tail(x_ref[...], z, ya_in, vn, s_of_vn, p_ref[...], vec, w_pa, w_pb,
                            w_out, w_pg, w_ple, final)


def _sample_kernel(x_ref, p_ref, st_ref, vec_ref, cw_ref, ws_ref, bs_ref, w_in, w_pa, w_pb, w_out,
                   w_pg, w_ple, xo_ref, xa_ref, vrow_ref, h_ref, acc_ref, *, final, nb):
    j = pl.program_id(0)
    vec = vec_ref[...]

    @pl.when(j == 0)
    def _():
        h = _bf(_rms(x_ref[...], _row(vec, V_NORM_G)))
        h_ref[...] = h
        glu = _dot(h, _w(w_in, C_GLU, C_ZA))
        xa = glu[:, :D] * _sigmoid(glu[:, D:])
        xa_ref[...] = xa
        acc_ref[...] = xa * cw_ref[HIST:KCONV, :] + _row(vec, V_CONV_B)

    part = acc_ref[...]
    for kk in range(KS):
        part = part + st_ref[kk] * cw_ref[pl.ds(j * KS + kk, 1), :]
    acc_ref[...] = part

    @pl.when(j == nb - 1)
    def _():
        lane_g = jax.lax.broadcasted_iota(jnp.int32, (1, D), 1) // HD
        w00 = jnp.zeros((1, D), jnp.float32)
        b0 = jnp.zeros((1, D), jnp.float32)
        for g in range(GROUPS):
            w00 = jnp.where(lane_g == g, ws_ref[g, 0:1, 0:1], w00)
            b0 = jnp.where(lane_g == g, bs_ref[0:1, g:g + 1], b0)

        def s_of_vn(vn):
            return _bf(w00).astype(jnp.float32) * _bf(vn).astype(jnp.float32) + b0

        h = h_ref[...]

        def z(c0, c1):
            return _dot(h, _w(w_in, c0, c1))

        ya_in, vn = _branch_inputs(acc_ref[...], z(C_ZA, C_U), z(C_V, C_ZB), vec)
        xo_ref[...] = _tail(x_ref[...], z, ya_in, vn, s_of_vn, p_ref[...], vec, w_pa, w_pb,
                            w_out, w_pg, w_ple, final)
        vrow_ref[...] = vn


def _state_kernel(st_ref, nxt_ref, xa_ref, o_ref, *, nb):
    j = pl.program_id(1)
    o_ref[0:KS - 1] = st_ref[1:KS]

    @pl.when(j < nb - 1)
    def _():
        o_ref[KS - 1] = nxt_ref[0]

    @pl.when(j == nb - 1)
    def _():
        o_ref[KS - 1] = xa_ref[...]


def _pack_kernel(w_ref, o_ref):
    o_ref[...] = pltpu.bitcast(_bf(w_ref[...]), jnp.uint32)


def _layer_block(layer, shape):
    nd = len(shape)
    return pl.BlockSpec((None,) + tuple(shape), lambda *_: (layer,) + (0,) * nd,
                        pipeline_mode=pl.Buffered(1))


def _shared_specs(layer):
    return [
        _layer_block(layer, (N_VEC, D)), _layer_block(layer, (CARRY, D)),
        _layer_block(layer, (GROUPS, CHUNK, CHUNK)), _layer_block(layer, (CHUNK, GROUPS)),
        _layer_block(layer, (D // 2, C_END)), _layer_block(layer, (D // 2, D)),
        _layer_block(layer, (D // 2, D)), _layer_block(layer, (D // 2, D)),
        _layer_block(layer, (D // 2, D)), _layer_block(layer, (PLE // 2, D)),
    ]


def _prompt_layer(layer, x, p, shared, final):
    b, t, _ = x.shape
    assert t % TM == 0 and TM % CHUNK == 0 and t >= HIST
    in_specs = [
        pl.BlockSpec((None, TM, D), lambda i, j: (i, j, 0)),
        pl.BlockSpec((None, None, TM, PLE), lambda i, j: (layer, i, j, 0)),
    ] + _shared_specs(layer)
    out_specs = [
        pl.BlockSpec((None, TM, D), lambda i, j: (i, j, 0)),
        pl.BlockSpec((None, HIST, D), lambda i, j: (i, 0, 0)),
        pl.BlockSpec((None, CHUNK, D), lambda i, j: (i, 0, 0)),
    ]
    out_shape = [
        jax.ShapeDtypeStruct((b, t, D), jnp.float32),
        jax.ShapeDtypeStruct((b, HIST, D), jnp.float32),
        jax.ShapeDtypeStruct((b, CHUNK, D), jnp.float32),
    ]
    return pl.pallas_call(
        functools.partial(_prompt_kernel, final=final),
        grid=(b, t // TM), in_specs=in_specs, out_specs=out_specs, out_shape=out_shape,
        scratch_shapes=[pltpu.VMEM((CARRY + TM, D), jnp.float32),
                        pltpu.VMEM((TM, C_END), jnp.float32),
                        pltpu.VMEM((TM, D), jnp.float32),
                        pltpu.VMEM((TM, D), jnp.bfloat16)],
        compiler_params=pltpu.CompilerParams(
            dimension_semantics=("arbitrary", "arbitrary"), vmem_limit_bytes=VMEM_LIMIT),
        name="prompt_layer",
    )(x, p, *shared)


def _sample_layer(layer, x, p, st, shared, final):
    n = x.shape[0]
    nb = HIST // KS
    in_specs = [
        pl.BlockSpec((n, D), lambda j: (0, 0), pipeline_mode=pl.Buffered(1)),
        _layer_block(layer, (n, PLE)),
        pl.BlockSpec((None, KS, n, D), lambda j: (layer, j, 0, 0)),
    ] + _shared_specs(layer)
    out_specs = [pl.BlockSpec((n, D), lambda j: (0, 0))] * 3
    out_shape = [jax.ShapeDtypeStruct((n, D), jnp.float32)] * 3
    return pl.pallas_call(
        functools.partial(_sample_kernel, final=final, nb=nb),
        grid=(nb,), in_specs=in_specs, out_specs=out_specs, out_shape=out_shape,
        scratch_shapes=[pltpu.VMEM((n, D), jnp.bfloat16), pltpu.VMEM((n, D), jnp.float32)],
        compiler_params=pltpu.CompilerParams(
            dimension_semantics=("arbitrary",), vmem_limit_bytes=VMEM_LIMIT),
        name="sample_layer",
    )(x, p, st, *shared)


def _shift_state(st, xa):
    depth, _, n, _ = st.shape
    nb = HIST // KS
    return pl.pallas_call(
        functools.partial(_state_kernel, nb=nb),
        grid=(depth, nb),
        in_specs=[
            pl.BlockSpec((None, KS, n, D), lambda i, j: (i, j, 0, 0)),
            pl.BlockSpec((None, 1, n, D),
                         lambda i, j: (i, jnp.minimum(KS * (j + 1), HIST - 1), 0, 0)),
            pl.BlockSpec((None, n, D), lambda i, j: (i, 0, 0)),
        ],
        out_specs=pl.BlockSpec((None, KS, n, D), lambda i, j: (i, j, 0, 0)),
        out_shape=jax.ShapeDtypeStruct(st.shape, jnp.float32),
        compiler_params=pltpu.CompilerParams(
            dimension_semantics=("arbitrary", "arbitrary"), vmem_limit_bytes=VMEM_LIMIT),
        name="shift_state",
    )(st, st, xa)


def _pack_rows(w):
    depth, k, n = w.shape
    nb = min(n, PACK_N)
    return pl.pallas_call(
        _pack_kernel,
        grid=(depth, n // nb),
        in_specs=[pl.BlockSpec((None, k, nb), lambda i, j: (i, 0, j))],
        out_specs=pl.BlockSpec((None, k // 2, nb), lambda i, j: (i, 0, j)),
        out_shape=jax.ShapeDtypeStruct((depth, k // 2, n), jnp.uint32),
        compiler_params=pltpu.CompilerParams(
            dimension_semantics=("arbitrary", "arbitrary"), vmem_limit_bytes=VMEM_LIMIT),
        name="pack_weights",
    )(w)


def kernel(x_prompt, x_sample, state_conv, p_prompt, p_sample, norm_g, w_in, conv_w, conv_b, ln_a_g, ln_a_b, w_proj_a, ln_v_g, ln_v_b, w_spatial, b_spatial, w_proj_b, w_out, ple_norm_g, w_ple_gate, b_ple_gate, w_ple, final_g):
    depth = w_in.shape[0]
    n_s = x_sample.shape[0]
    assert x_sample.shape[1] == 1
    xp = x_prompt
    xs = x_sample.reshape(n_s, D)
    ps = p_sample.reshape(depth, n_s, PLE)
    final_rows = jnp.broadcast_to(final_g[None, :], (depth, D))
    pad_rows = jnp.zeros((depth, N_VEC - 9, D), jnp.float32)
    vecs = jnp.concatenate(
        [jnp.stack([norm_g, conv_b, ln_a_g, ln_a_b, ln_v_g, ln_v_b, ple_norm_g, b_ple_gate,
                    final_rows], axis=1), pad_rows], axis=1)
    cws = jnp.pad(conv_w, ((0, 0), (0, CARRY - KCONV), (0, 0)))
    bst = jnp.swapaxes(b_spatial, 1, 2)
    shared = (vecs, cws, w_spatial, bst) + tuple(
        _pack_rows(w) for w in (w_in, w_proj_a, w_proj_b, w_out, w_ple_gate, w_ple))
    st = jnp.swapaxes(state_conv, 1, 2)
    conv_p, xa_s, vrow_p, vrow_s = [], [], [], []
    for i in range(depth):
        final = i == depth - 1
        xp, cp, vp = _prompt_layer(i, xp, p_prompt, shared, final)
        xs, xa, vs = _sample_layer(i, xs, ps, st, shared, final)
        conv_p.append(cp)
        xa_s.append(xa)
        vrow_p.append(vp)
        vrow_s.append(vs)
    conv_s = jnp.swapaxes(_shift_state(st, jnp.stack(xa_s)), 1, 2)
    return (xp, xs.reshape(n_s, 1, D), jnp.stack(conv_p), conv_s,
            jnp.stack(vrow_p), jnp.stack(vrow_s).reshape(depth, n_s, 1, D))
```

```python
import functools
import itertools

import jax
import jax.numpy as jnp
from jax.experimental import pallas as pl
from jax.experimental.pallas import tpu as pltpu

D = 1024
KCONV = 31
HIST = KCONV - 1
CHUNK = 128
GROUPS = 8
HD = D // GROUPS
PLE = 256
EPS = 1e-6
C_GLU, C_ZA, C_U, C_V, C_ZB, C_GA, C_GB, C_END = 0, 2048, 3072, 4096, 5120, 6144, 7168, 8192

TM = 256
CARRY = 32
SUB = 8
LEAD = CARRY - HIST
LANE = 128
MXU_N = 256
RB = 32
COST_GATE, COST_CONV = 256, 330
KS = 6
PACK_N = 2048
VMEM_LIMIT = 56 * 1024 * 1024

V_NORM_G, V_CONV_B, V_LNA_G, V_LNA_B, V_LNV_G, V_LNV_B, V_PLE_G, V_PLE_B, V_FINAL_G = range(9)
N_VEC = 16
N_BIG_W = 5


def _bf(x):
    return x.astype(jnp.bfloat16)


def _dot(a, b):
    return jnp.dot(a, b, preferred_element_type=jnp.float32)


def _w(ref, c0=None, c1=None):
    words = ref[...] if c0 is None else ref[:, c0:c1]
    return pltpu.bitcast(words, jnp.bfloat16)


def _rms(x, g):
    return x * jax.lax.rsqrt(jnp.mean(x * x, axis=-1, keepdims=True) + EPS) * g


def _ln(x, g, b):
    mu = jnp.mean(x, axis=-1, keepdims=True)
    xc = x - mu
    var = jnp.mean(xc * xc, axis=-1, keepdims=True)
    return xc * jax.lax.rsqrt(var + EPS) * g + b


def _sigmoid(x):
    return jax.nn.sigmoid(x)


def _silu(x):
    return x * jax.nn.sigmoid(x)


def _row(vec, i):
    return vec[i:i + 1]


def _zero_row_of(v):
    bits = jax.lax.bitcast_convert_type(v, jnp.uint32)
    zero = jax.lax.shift_right_logical(jax.lax.shift_right_logical(bits, jnp.uint32(16)),
                                       jnp.uint32(16))
    return jax.lax.bitcast_convert_type(zero, jnp.float32)[0:1, :]


def _conv_lane_tile(xx_ref, cw_ref, acc_ref, bias, l0, take_zero):
    lanes = slice(l0, l0 + LANE)
    w = [cw_ref[k:k + 1, lanes] for k in range(KCONV)]

    def partial(r, row0, rows):
        part = None
        for q in range((KCONV + LEAD + SUB - 1) // SUB):
            k = SUB * q + r - LEAD
            if 0 <= k < KCONV:
                term = xx_ref[row0 + SUB * q:row0 + SUB * q + rows, lanes] * w[k]
                part = term if part is None else part + term
        return part

    tails = {r: partial(r, 0, SUB) for r in range(1, SUB)}
    for r0 in range(0, TM, RB):
        acc = partial(0, r0, RB) + bias[:, lanes]
        zero = take_zero()
        if zero is not None:
            acc = acc + zero
        for r in range(1, SUB):
            new = partial(r, r0 + SUB, RB)
            window = jnp.concatenate([tails[r], new], axis=0)
            acc = acc + pltpu.roll(window, RB + SUB - r, axis=0)[0:RB]
            tails[r] = new[RB - SUB:]
        acc_ref[r0:r0 + RB, lanes] = acc
        yield
    xx_ref[0:CARRY, lanes] = xx_ref[TM:TM + CARRY, lanes]
    yield


def _branch_inputs(conv, za, v, vec):
    ya = _silu(_ln(conv, _row(vec, V_LNA_G), _row(vec, V_LNA_B))) * _silu(za)
    return _bf(ya), _ln(v, _row(vec, V_LNV_G), _row(vec, V_LNV_B))


def _tail(x, z, ya_in, vn, s_of_vn, p, vec, w_pa, w_pb, w_out, w_pg, w_ple, final):
    ya = _dot(ya_in, _w(w_pa))
    yb = _dot(_bf(z(C_U, C_V) * s_of_vn(vn) * _silu(z(C_ZB, C_GA))), _w(w_pb))
    m = _sigmoid(z(C_GA, C_GB)) * ya + _sigmoid(z(C_GB, C_END)) * yb
    x = x + _dot(_bf(m), _w(w_out))
    gate = _dot(_bf(_rms(x, _row(vec, V_PLE_G))), _w(w_pg)) + _row(vec, V_PLE_B)
    x = x + _sigmoid(gate) * _dot(_bf(p), _w(w_ple))
    if final:
        x = _rms(x, _row(vec, V_FINAL_G))
    return x


def _prompt_kernel(x_ref, p_ref, vec_ref, cw_ref, ws_ref, bs_ref, w_in, w_pa, w_pb, w_out, w_pg,
                   w_ple, *rest, final):
    n_next = 0 if final else N_BIG_W
    next_f32 = rest[:n_next]
    xo_ref, conv_ref, vrow_ref = rest[n_next:n_next + 3]
    next_packed = rest[n_next + 3:2 * n_next + 3]
    xx_ref, z_ref, acc_ref, h_ref = rest[2 * n_next + 3:]
    j = pl.program_id(1)

    @pl.when(j == 0)
    def _():
        xx_ref[0:CARRY, :] = jnp.zeros((CARRY, D), jnp.float32)

    vec = vec_ref[...]
    h_ref[...] = _bf(_rms(x_ref[...], _row(vec, V_NORM_G)))

    bias = _row(vec, V_CONV_B)
    popped = []
    in_flight = []

    def proj_piece(c0):
        res = _dot(h_ref[...], _w(w_in, c0, c0 + MXU_N))
        z_ref[:, c0:c0 + MXU_N] = res
        popped.extend(in_flight)
        in_flight[:] = [_zero_row_of(res[0:SUB, 0:LANE])]

    def take_zero():
        zero = None
        while popped:
            row = popped.pop()
            zero = row if zero is None else zero + row
        return zero

    def interleave(pieces, blocks, total_cost, needed):
        issued = spent = n = 0
        while True:
            while issued < len(pieces) and (issued < needed(n)
                                            or issued * total_cost <= spent * len(pieces)):
                proj_piece(pieces[issued])
                issued += 1
            cost = next(blocks, None)
            if cost is None:
                break
            spent += cost
            n += 1
        assert spent == total_cost and issued == len(pieces)

    def lane_tile_work(l0):
        lanes = slice(l0, l0 + LANE)
        gate = z_ref[:, D + l0:D + l0 + LANE]
        xx_ref[CARRY:CARRY + TM, lanes] = z_ref[:, lanes] * _sigmoid(gate)
        yield COST_GATE
        for _ in _conv_lane_tile(xx_ref, cw_ref, acc_ref, bias, l0, take_zero):
            yield COST_CONV

    glu_cols = [c for c0 in range(0, D, MXU_N) for c in (c0, D + c0)]
    tiles = list(range(0, D, LANE))
    blocks_per_tile = TM // RB + 2
    interleave(glu_cols + list(range(C_ZA, C_END, MXU_N)),
               itertools.chain(*[lane_tile_work(l0) for l0 in tiles]),
               len(tiles) * (COST_GATE + (blocks_per_tile - 1) * COST_CONV),
               lambda n: 2 * (tiles[min(n // blocks_per_tile, len(tiles) - 1)] // MXU_N) + 2)

    conv_ref[...] = xx_ref[CARRY + TM - HIST:CARRY + TM, :]

    row = jax.lax.broadcasted_iota(jnp.int32, (CHUNK, CHUNK), 0)
    col = jax.lax.broadcasted_iota(jnp.int32, (CHUNK, CHUNK), 1)
    causal = row >= col
    bs = bs_ref[...]
    n_ch = TM // CHUNK

    def s_of_vn(vn):
        vb = _bf(vn)
        cols = []
        for g in range(GROUPS):
            wm = _bf(jnp.where(causal, ws_ref[g], 0.0))
            rhs = jnp.concatenate([vb[c * CHUNK:(c + 1) * CHUNK, g * HD:(g + 1) * HD]
                                   for c in range(n_ch)], axis=1)
            res = _dot(wm, rhs) + bs[:, g:g + 1]
            cols.append(jnp.concatenate([res[:, c * HD:(c + 1) * HD] for c in range(n_ch)],
                                        axis=0))
        return jnp.concatenate(cols, axis=1)

    def z(c0, c1):
        return z_ref[:, c0:c1]

    @pl.when(j >= 0)
    def _():
        for src, dst in zip(next_f32, next_packed):
            dst[...] = pltpu.bitcast(_bf(src[...]), jnp.uint32)
        ya_in, vn = _branch_inputs(acc_ref[...], z(C_ZA, C_U), z(C_V, C_ZB), vec)
        vrow_ref[...] = vn[TM - CHUNK:, :]
        xo_ref[...] = _tail(x_ref[...], z, ya_in, vn, s_of_vn, p_ref[...], vec, w_pa, w_pb,
                            w_out, w_pg, w_ple, final)


def _sample_kernel(x_ref, p_ref, st_ref, vec_ref, cw_ref, ws_ref, bs_ref, w_in, w_pa, w_pb, w_out,
                   w_pg, w_ple, xo_ref, xa_ref, vrow_ref, h_ref, acc_ref, *, final, nb):
    j = pl.program_id(0)
    vec = vec_ref[...]

    @pl.when(j == 0)
    def _():
        h = _bf(_rms(x_ref[...], _row(vec, V_NORM_G)))
        h_ref[...] = h
        glu = _dot(h, _w(w_in, C_GLU, C_ZA))
        xa = glu[:, :D] * _sigmoid(glu[:, D:])
        xa_ref[...] = xa
        acc_ref[...] = xa * cw_ref[HIST:KCONV, :] + _row(vec, V_CONV_B)

    part = acc_ref[...]
    for kk in range(KS):
        part = part + st_ref[kk] * cw_ref[pl.ds(j * KS + kk, 1), :]
    acc_ref[...] = part

    @pl.when(j == nb - 1)
    def _():
        lane_g = jax.lax.broadcasted_iota(jnp.int32, (1, D), 1) // HD
        w00 = jnp.zeros((1, D), jnp.float32)
        b0 = jnp.zeros((1, D), jnp.float32)
        for g in range(GROUPS):
            w00 = jnp.where(lane_g == g, ws_ref[g, 0:1, 0:1], w00)
            b0 = jnp.where(lane_g == g, bs_ref[0:1, g:g + 1], b0)

        def s_of_vn(vn):
            return _bf(w00).astype(jnp.float32) * _bf(vn).astype(jnp.float32) + b0

        h = h_ref[...]

        def z(c0, c1):
            return _dot(h, _w(w_in, c0, c1))

        ya_in, vn = _branch_inputs(acc_ref[...], z(C_ZA, C_U), z(C_V, C_ZB), vec)
        xo_ref[...] = _tail(x_ref[...], z, ya_in, vn, s_of_vn, p_ref[...], vec, w_pa, w_pb,
                            w_out, w_pg, w_ple, final)
        vrow_ref[...] = vn


def _state_kernel(st_ref, nxt_ref, xa_ref, o_ref, *, nb):
    j = pl.program_id(1)
    o_ref[0:KS - 1] = st_ref[1:KS]

    @pl.when(j < nb - 1)
    def _():
        o_ref[KS - 1] = nxt_ref[0]

    @pl.when(j == nb - 1)
    def _():
        o_ref[KS - 1] = xa_ref[...]


def _pack_kernel(w_ref, o_ref):
    o_ref[...] = pltpu.bitcast(_bf(w_ref[...]), jnp.uint32)


def _layer_block(layer, shape):
    nd = len(shape)
    return pl.BlockSpec((None,) + tuple(shape), lambda *_: (layer,) + (0,) * nd,
                        pipeline_mode=pl.Buffered(1))


def _resident(shape):
    nd = len(shape)
    return pl.BlockSpec(tuple(shape), lambda *_: (0,) * nd, pipeline_mode=pl.Buffered(1))


def _shared_specs(layer):
    return [
        _layer_block(layer, (N_VEC, D)), _layer_block(layer, (CARRY, D)),
        _layer_block(layer, (GROUPS, CHUNK, CHUNK)), _layer_block(layer, (CHUNK, GROUPS)),
        _resident((D // 2, C_END)), _resident((D // 2, D)), _resident((D // 2, D)),
        _resident((D // 2, D)), _resident((D // 2, D)), _layer_block(layer, (PLE // 2, D)),
    ]


def _prompt_layer(layer, x, p, shared, next_f32):
    b, t, _ = x.shape
    assert t % TM == 0 and TM % CHUNK == 0 and t >= HIST
    n_j = t // TM
    steps = b * n_j
    in_specs = [
        pl.BlockSpec((None, TM, D), lambda i, j: (i, j, 0)),
        pl.BlockSpec((None, None, TM, PLE), lambda i, j: (layer, i, j, 0)),
    ] + _shared_specs(layer)
    out_specs = [
        pl.BlockSpec((None, TM, D), lambda i, j: (i, j, 0)),
        pl.BlockSpec((None, HIST, D), lambda i, j: (i, 0, 0)),
        pl.BlockSpec((None, CHUNK, D), lambda i, j: (i, 0, 0)),
    ]
    out_shape = [
        jax.ShapeDtypeStruct((b, t, D), jnp.float32),
        jax.ShapeDtypeStruct((b, HIST, D), jnp.float32),
        jax.ShapeDtypeStruct((b, CHUNK, D), jnp.float32),
    ]
    for w in next_f32:
        _, k, n = w.shape
        rows = k // steps
        assert rows * steps == k and rows % (2 * SUB) == 0
        in_specs.append(pl.BlockSpec((None, rows, n), lambda i, j: (layer + 1, i * n_j + j, 0)))
        out_specs.append(pl.BlockSpec((rows // 2, n), lambda i, j: (i * n_j + j, 0)))
        out_shape.append(jax.ShapeDtypeStruct((k // 2, n), jnp.uint32))
    return pl.pallas_call(
        functools.partial(_prompt_kernel, final=not next_f32),
        grid=(b, n_j), in_specs=in_specs, out_specs=out_specs, out_shape=out_shape,
        scratch_shapes=[pltpu.VMEM((CARRY + TM, D), jnp.float32),
                        pltpu.VMEM((TM, C_END), jnp.float32),
                        pltpu.VMEM((TM, D), jnp.float32),
                        pltpu.VMEM((TM, D), jnp.bfloat16)],
        compiler_params=pltpu.CompilerParams(
            dimension_semantics=("arbitrary", "arbitrary"), vmem_limit_bytes=VMEM_LIMIT),
        name="prompt_layer",
    )(x, p, *shared, *next_f32)


def _sample_layer(layer, x, p, st, shared, final):
    n = x.shape[0]
    nb = HIST // KS
    in_specs = [
        pl.BlockSpec((n, D), lambda j: (0, 0), pipeline_mode=pl.Buffered(1)),
        _layer_block(layer, (n, PLE)),
        pl.BlockSpec((None, KS, n, D), lambda j: (layer, j, 0, 0)),
    ] + _shared_specs(layer)
    out_specs = [pl.BlockSpec((n, D), lambda j: (0, 0))] * 3
    out_shape = [jax.ShapeDtypeStruct((n, D), jnp.float32)] * 3
    return pl.pallas_call(
        functools.partial(_sample_kernel, final=final, nb=nb),
        grid=(nb,), in_specs=in_specs, out_specs=out_specs, out_shape=out_shape,
        scratch_shapes=[pltpu.VMEM((n, D), jnp.bfloat16), pltpu.VMEM((n, D), jnp.float32)],
        compiler_params=pltpu.CompilerParams(
            dimension_semantics=("arbitrary",), vmem_limit_bytes=VMEM_LIMIT),
        name="sample_layer",
    )(x, p, st, *shared)


def _shift_state(st, xa):
    depth, _, n, _ = st.shape
    nb = HIST // KS
    return pl.pallas_call(
        functools.partial(_state_kernel, nb=nb),
        grid=(depth, nb),
        in_specs=[
            pl.BlockSpec((None, KS, n, D), lambda i, j: (i, j, 0, 0)),
            pl.BlockSpec((None, 1, n, D),
                         lambda i, j: (i, jnp.minimum(KS * (j + 1), HIST - 1), 0, 0)),
            pl.BlockSpec((None, n, D), lambda i, j: (i, 0, 0)),
        ],
        out_specs=pl.BlockSpec((None, KS, n, D), lambda i, j: (i, j, 0, 0)),
        out_shape=jax.ShapeDtypeStruct(st.shape, jnp.float32),
        compiler_params=pltpu.CompilerParams(
            dimension_semantics=("arbitrary", "arbitrary"), vmem_limit_bytes=VMEM_LIMIT),
        name="shift_state",
    )(st, st, xa)


def _pack_rows(w, layers):
    _, k, n = w.shape
    nb = min(n, PACK_N)
    return pl.pallas_call(
        _pack_kernel,
        grid=(layers, n // nb),
        in_specs=[pl.BlockSpec((None, k, nb), lambda i, j: (i, 0, j))],
        out_specs=pl.BlockSpec((None, k // 2, nb), lambda i, j: (i, 0, j)),
        out_shape=jax.ShapeDtypeStruct((layers, k // 2, n), jnp.uint32),
        compiler_params=pltpu.CompilerParams(
            dimension_semantics=("arbitrary", "arbitrary"), vmem_limit_bytes=VMEM_LIMIT),
        name="pack_weights",
    )(w)


def kernel(x_prompt, x_sample, state_conv, p_prompt, p_sample, norm_g, w_in, conv_w, conv_b, ln_a_g, ln_a_b, w_proj_a, ln_v_g, ln_v_b, w_spatial, b_spatial, w_proj_b, w_out, ple_norm_g, w_ple_gate, b_ple_gate, w_ple, final_g):
    depth = w_in.shape[0]
    n_s = x_sample.shape[0]
    assert x_sample.shape[1] == 1
    xp = x_prompt
    xs = x_sample.reshape(n_s, D)
    ps = p_sample.reshape(depth, n_s, PLE)
    final_rows = jnp.broadcast_to(final_g[None, :], (depth, D))
    pad_rows = jnp.zeros((depth, N_VEC - 9, D), jnp.float32)
    vecs = jnp.concatenate(
        [jnp.stack([norm_g, conv_b, ln_a_g, ln_a_b, ln_v_g, ln_v_b, ple_norm_g, b_ple_gate,
                    final_rows], axis=1), pad_rows], axis=1)
    cws = jnp.pad(conv_w, ((0, 0), (0, CARRY - KCONV), (0, 0)))
    bst = jnp.swapaxes(b_spatial, 1, 2)
    big_w = (w_in, w_proj_a, w_proj_b, w_out, w_ple_gate)
    assert len(big_w) == N_BIG_W
    packed = tuple(_pack_rows(w, 1)[0] for w in big_w)
    w_ple_packed = _pack_rows(w_ple, depth)
    st = jnp.swapaxes(state_conv, 1, 2)
    conv_p, xa_s, vrow_p, vrow_s = [], [], [], []
    for i in range(depth):
        final = i == depth - 1
        shared = (vecs, cws, w_spatial, bst) + packed + (w_ple_packed,)
        xp, cp, vp, *packed_next = _prompt_layer(i, xp, p_prompt, shared, () if final else big_w)
        xs, xa, vs = _sample_layer(i, xs, ps, st, shared, final)
        packed = tuple(packed_next)
        conv_p.append(cp)
        xa_s.append(xa)
        vrow_p.append(vp)
        vrow_s.append(vs)
    conv_s = jnp.swapaxes(_shift_state(st, jnp.stack(xa_s)), 1, 2)
    return (xp, xs.reshape(n_s, 1, D), jnp.stack(conv_p), conv_s,
            jnp.stack(vrow_p), jnp.stack(vrow_s).reshape(depth, n_s, 1, D))
```

```python
import functools
import itertools

import jax
import jax.numpy as jnp
from jax.experimental import pallas as pl
from jax.experimental.pallas import tpu as pltpu

D = 1024
KCONV = 31
HIST = KCONV - 1
CHUNK = 128
GROUPS = 8
HD = D // GROUPS
PLE = 256
EPS = 1e-6
C_GLU, C_ZA, C_U, C_V, C_ZB, C_GA, C_GB, C_END = 0, 2048, 3072, 4096, 5120, 6144, 7168, 8192

TM = 256
CARRY = 32
SUB = 8
LEAD = CARRY - HIST
LANE = 128
MXU_N = 256
RB = 32
COST_GATE, COST_CONV, COST_ACT = 256, 330, 224
KS = 10
PACK_N = 2048
VMEM_LIMIT = 56 * 1024 * 1024

V_NORM_G, V_CONV_B, V_LNA_G, V_LNA_B, V_LNV_G, V_LNV_B, V_PLE_G, V_PLE_B, V_FINAL_G = range(9)
N_VEC = 16
N_BIG_W = 5


def _bf(x):
    return x.astype(jnp.bfloat16)


def _dot(a, b):
    return jnp.dot(a, b, preferred_element_type=jnp.float32)


def _w(ref, c0=None, c1=None):
    words = ref[...] if c0 is None else ref[:, c0:c1]
    return pltpu.bitcast(words, jnp.bfloat16)


def _rms(x, g):
    return x * jax.lax.rsqrt(jnp.mean(x * x, axis=-1, keepdims=True) + EPS) * g


def _ln(x, g, b):
    mu = jnp.mean(x, axis=-1, keepdims=True)
    xc = x - mu
    var = jnp.mean(xc * xc, axis=-1, keepdims=True)
    return xc * jax.lax.rsqrt(var + EPS) * g + b


def _sigmoid(x):
    return jax.nn.sigmoid(x)


def _silu(x):
    return x * jax.nn.sigmoid(x)


def _row(vec, i):
    return vec[i:i + 1]


def _zero_row_of(v):
    bits = jax.lax.bitcast_convert_type(v, jnp.uint32)
    zero = jax.lax.shift_right_logical(jax.lax.shift_right_logical(bits, jnp.uint32(16)),
                                       jnp.uint32(16))
    return jax.lax.bitcast_convert_type(zero, jnp.float32)[0:1, :]


def _conv_lane_tile(xx_ref, cw_ref, acc_ref, bias, l0, take_zero):
    lanes = slice(l0, l0 + LANE)
    w = [cw_ref[k:k + 1, lanes] for k in range(KCONV)]

    def partial(r, row0, rows):
        part = None
        for q in range((KCONV + LEAD + SUB - 1) // SUB):
            k = SUB * q + r - LEAD
            if 0 <= k < KCONV:
                term = xx_ref[row0 + SUB * q:row0 + SUB * q + rows, lanes] * w[k]
                part = term if part is None else part + term
        return part

    tails = {r: partial(r, 0, SUB) for r in range(1, SUB)}
    for r0 in range(0, TM, RB):
        acc = partial(0, r0, RB) + bias[:, lanes]
        zero = take_zero()
        if zero is not None:
            acc = acc + zero
        for r in range(1, SUB):
            new = partial(r, r0 + SUB, RB)
            window = jnp.concatenate([tails[r], new], axis=0)
            acc = acc + pltpu.roll(window, RB + SUB - r, axis=0)[0:RB]
            tails[r] = new[RB - SUB:]
        acc_ref[r0:r0 + RB, lanes] = acc
        yield
    xx_ref[0:CARRY, lanes] = xx_ref[TM:TM + CARRY, lanes]
    yield


def _branch_inputs(conv, za, v, vec, za_done=False):
    ya = _silu(_ln(conv, _row(vec, V_LNA_G), _row(vec, V_LNA_B))) * (za if za_done else _silu(za))
    return _bf(ya), _ln(v, _row(vec, V_LNV_G), _row(vec, V_LNV_B))


def _tail(x, z, ya_in, vn, s_of_vn, p, vec, w_pa, w_pb, w_out, w_pg, w_ple, final,
          u_done=False):
    ya = _dot(ya_in, _w(w_pa))
    ug = z(C_U, C_V) if u_done else z(C_U, C_V) * _silu(z(C_ZB, C_GA))
    yb = _dot(_bf(ug * s_of_vn(vn)), _w(w_pb))
    m = _sigmoid(z(C_GA, C_GB)) * ya + _sigmoid(z(C_GB, C_END)) * yb
    x = x + _dot(_bf(m), _w(w_out))
    gate = _dot(_bf(_rms(x, _row(vec, V_PLE_G))), _w(w_pg)) + _row(vec, V_PLE_B)
    x = x + _sigmoid(gate) * _dot(_bf(p), _w(w_ple))
    if final:
        x = _rms(x, _row(vec, V_FINAL_G))
    return x


def _prompt_kernel(x_ref, p_ref, vec_ref, cw_ref, ws_ref, bs_ref, w_in, w_pa, w_pb, w_out, w_pg,
                   w_ple, *rest, final):
    n_next = 0 if final else N_BIG_W
    next_f32 = rest[:n_next]
    xo_ref, conv_ref, vrow_ref = rest[n_next:n_next + 3]
    next_packed = rest[n_next + 3:2 * n_next + 3]
    xx_ref, z_ref, acc_ref, h_ref = rest[2 * n_next + 3:]
    j = pl.program_id(1)

    @pl.when(j == 0)
    def _():
        xx_ref[0:CARRY, :] = jnp.zeros((CARRY, D), jnp.float32)

    vec = vec_ref[...]
    h_ref[...] = _bf(_rms(x_ref[...], _row(vec, V_NORM_G)))

    bias = _row(vec, V_CONV_B)
    popped = []
    in_flight = []

    def proj_piece(c0):
        res = _dot(h_ref[...], _w(w_in, c0, c0 + MXU_N))
        z_ref[:, c0:c0 + MXU_N] = res
        popped.extend(in_flight)
        in_flight[:] = [_zero_row_of(res[0:SUB, 0:LANE])]

    def take_zero():
        zero = None
        while popped:
            row = popped.pop()
            zero = row if zero is None else zero + row
        return zero

    def interleave(pieces, blocks, total_cost, needed):
        issued = spent = n = 0
        while True:
            while issued < len(pieces) and (issued < needed(n)
                                            or issued * total_cost <= spent * len(pieces)):
                proj_piece(pieces[issued])
                issued += 1
            cost = next(blocks, None)
            if cost is None:
                break
            spent += cost
            n += 1
        assert spent == total_cost and issued == len(pieces)

    def lane_tile_work(l0):
        lanes = slice(l0, l0 + LANE)
        gate = z_ref[:, D + l0:D + l0 + LANE]
        xx_ref[CARRY:CARRY + TM, lanes] = z_ref[:, lanes] * _sigmoid(gate)
        yield COST_GATE
        for _ in _conv_lane_tile(xx_ref, cw_ref, acc_ref, bias, l0, take_zero):
            yield COST_CONV

    def act_work():
        for r0 in range(0, TM, RB):
            rows = slice(r0, r0 + RB)
            z_ref[rows, C_ZA:C_U] = _silu(z_ref[rows, C_ZA:C_U])
            z_ref[rows, C_U:C_V] = z_ref[rows, C_U:C_V] * _silu(z_ref[rows, C_ZB:C_GA])
            yield COST_ACT

    glu_cols = [c for c0 in range(0, D, MXU_N) for c in (c0, D + c0)]
    tiles = list(range(0, D, LANE))
    blocks_per_tile = TM // RB + 2
    n_conv = len(tiles) * blocks_per_tile

    def needed(n):
        if n < n_conv:
            return 2 * (tiles[n // blocks_per_tile] // MXU_N) + 2
        return len(glu_cols) + (C_GA - C_ZA) // MXU_N

    interleave(glu_cols + list(range(C_ZA, C_END, MXU_N)),
               itertools.chain(*[lane_tile_work(l0) for l0 in tiles], act_work()),
               len(tiles) * (COST_GATE + (blocks_per_tile - 1) * COST_CONV)
               + (TM // RB) * COST_ACT, needed)

    conv_ref[...] = xx_ref[CARRY + TM - HIST:CARRY + TM, :]

    row = jax.lax.broadcasted_iota(jnp.int32, (CHUNK, CHUNK), 0)
    col = jax.lax.broadcasted_iota(jnp.int32, (CHUNK, CHUNK), 1)
    causal = row >= col
    bs = bs_ref[...]
    n_ch = TM // CHUNK

    def s_of_vn(vn):
        vb = _bf(vn)
        cols = []
        for g in range(GROUPS):
            wm = _bf(jnp.where(causal, ws_ref[g], 0.0))
            rhs = jnp.concatenate([vb[c * CHUNK:(c + 1) * CHUNK, g * HD:(g + 1) * HD]
                                   for c in range(n_ch)], axis=1)
            res = _dot(wm, rhs) + bs[:, g:g + 1]
            cols.append(jnp.concatenate([res[:, c * HD:(c + 1) * HD] for c in range(n_ch)],
                                        axis=0))
        return jnp.concatenate(cols, axis=1)

    def z(c0, c1):
        return z_ref[:, c0:c1]

    @pl.when(j >= 0)
    def _():
        for src, dst in zip(next_f32, next_packed):
            dst[...] = pltpu.bitcast(_bf(src[...]), jnp.uint32)
        ya_in, vn = _branch_inputs(acc_ref[...], z(C_ZA, C_U), z(C_V, C_ZB), vec, za_done=True)
        vrow_ref[...] = vn[TM - CHUNK:, :]
        xo_ref[...] = _tail(x_ref[...], z, ya_in, vn, s_of_vn, p_ref[...], vec, w_pa, w_pb,
                            w_out, w_pg, w_ple, final, u_done=True)


def _sample_kernel(x_ref, p_ref, st_ref, vec_ref, cw_ref, ws_ref, bs_ref, w_in, w_pa, w_pb, w_out,
                   w_pg, w_ple, xo_ref, xa_ref, vrow_ref, h_ref, acc_ref, *, final, nb):
    j = pl.program_id(0)
    vec = vec_ref[...]

    @pl.when(j == 0)
    def _():
        h = _bf(_rms(x_ref[...], _row(vec, V_NORM_G)))
        h_ref[...] = h
        glu = _dot(h, _w(w_in, C_GLU, C_ZA))
        xa = glu[:, :D] * _sigmoid(glu[:, D:])
        xa_ref[...] = xa
        acc_ref[...] = xa * cw_ref[HIST:KCONV, :] + _row(vec, V_CONV_B)

    part = acc_ref[...]
    for kk in range(KS):
        part = part + st_ref[kk] * cw_ref[pl.ds(j * KS + kk, 1), :]
    acc_ref[...] = part

    @pl.when(j == nb - 1)
    def _():
        lane_g = jax.lax.broadcasted_iota(jnp.int32, (1, D), 1) // HD
        w00 = jnp.zeros((1, D), jnp.float32)
        b0 = jnp.zeros((1, D), jnp.float32)
        for g in range(GROUPS):
            w00 = jnp.where(lane_g == g, ws_ref[g, 0:1, 0:1], w00)
            b0 = jnp.where(lane_g == g, bs_ref[0:1, g:g + 1], b0)

        def s_of_vn(vn):
            return _bf(w00).astype(jnp.float32) * _bf(vn).astype(jnp.float32) + b0

        h = h_ref[...]

        def z(c0, c1):
            return _dot(h, _w(w_in, c0, c1))

        ya_in, vn = _branch_inputs(acc_ref[...], z(C_ZA, C_U), z(C_V, C_ZB), vec)
        xo_ref[...] = _tail(x_ref[...], z, ya_in, vn, s_of_vn, p_ref[...], vec, w_pa, w_pb,
                            w_out, w_pg, w_ple, final)
        vrow_ref[...] = vn


def _state_kernel(st_ref, nxt_ref, xa_ref, o_ref, *, nb):
    j = pl.program_id(1)
    o_ref[0:KS - 1] = st_ref[1:KS]

    @pl.when(j < nb - 1)
    def _():
        o_ref[KS - 1] = nxt_ref[0]

    @pl.when(j == nb - 1)
    def _():
        o_ref[KS - 1] = xa_ref[...]


def _pack_kernel(w_ref, o_ref):
    o_ref[...] = pltpu.bitcast(_bf(w_ref[...]), jnp.uint32)


def _layer_block(layer, shape):
    nd = len(shape)
    return pl.BlockSpec((None,) + tuple(shape), lambda *_: (layer,) + (0,) * nd,
                        pipeline_mode=pl.Buffered(1))


def _resident(shape):
    nd = len(shape)
    return pl.BlockSpec(tuple(shape), lambda *_: (0,) * nd, pipeline_mode=pl.Buffered(1))


def _shared_specs(layer):
    return [
        _layer_block(layer, (N_VEC, D)), _layer_block(layer, (CARRY, D)),
        _layer_block(layer, (GROUPS, CHUNK, CHUNK)), _layer_block(layer, (CHUNK, GROUPS)),
        _resident((D // 2, C_END)), _resident((D // 2, D)), _resident((D // 2, D)),
        _resident((D // 2, D)), _resident((D // 2, D)), _layer_block(layer, (PLE // 2, D)),
    ]


def _prompt_layer(layer, x, p, shared, next_f32):
    b, t, _ = x.shape
    assert t % TM == 0 and TM % CHUNK == 0 and t >= HIST
    n_j = t // TM
    steps = b * n_j
    in_specs = [
        pl.BlockSpec((None, TM, D), lambda i, j: (i, j, 0)),
        pl.BlockSpec((None, None, TM, PLE), lambda i, j: (layer, i, j, 0)),
    ] + _shared_specs(layer)
    out_specs = [
        pl.BlockSpec((None, TM, D), lambda i, j: (i, j, 0)),
        pl.BlockSpec((None, HIST, D), lambda i, j: (i, 0, 0)),
        pl.BlockSpec((None, CHUNK, D), lambda i, j: (i, 0, 0)),
    ]
    out_shape = [
        jax.ShapeDtypeStruct((b, t, D), jnp.float32),
        jax.ShapeDtypeStruct((b, HIST, D), jnp.float32),
        jax.ShapeDtypeStruct((b, CHUNK, D), jnp.float32),
    ]
    for w in next_f32:
        _, k, n = w.shape
        rows = k // steps
        assert rows * steps == k and rows % (2 * SUB) == 0
        in_specs.append(pl.BlockSpec((None, rows, n), lambda i, j: (layer + 1, i * n_j + j, 0)))
        out_specs.append(pl.BlockSpec((rows // 2, n), lambda i, j: (i * n_j + j, 0)))
        out_shape.append(jax.ShapeDtypeStruct((k // 2, n), jnp.uint32))
    return pl.pallas_call(
        functools.partial(_prompt_kernel, final=not next_f32),
        grid=(b, n_j), in_specs=in_specs, out_specs=out_specs, out_shape=out_shape,
        scratch_shapes=[pltpu.VMEM((CARRY + TM, D), jnp.float32),
                        pltpu.VMEM((TM, C_END), jnp.float32),
                        pltpu.VMEM((TM, D), jnp.float32),
                        pltpu.VMEM((TM, D), jnp.bfloat16)],
        compiler_params=pltpu.CompilerParams(
            dimension_semantics=("arbitrary", "arbitrary"), vmem_limit_bytes=VMEM_LIMIT),
        name="prompt_layer",
    )(x, p, *shared, *next_f32)


def _sample_layer(layer, x, p, st, shared, final):
    n = x.shape[0]
    nb = HIST // KS
    in_specs = [
        pl.BlockSpec((n, D), lambda j: (0, 0), pipeline_mode=pl.Buffered(1)),
        _layer_block(layer, (n, PLE)),
        pl.BlockSpec((None, KS, n, D), lambda j: (layer, j, 0, 0)),
    ] + _shared_specs(layer)
    out_specs = [pl.BlockSpec((n, D), lambda j: (0, 0))] * 3
    out_shape = [jax.ShapeDtypeStruct((n, D), jnp.float32)] * 3
    return pl.pallas_call(
        functools.partial(_sample_kernel, final=final, nb=nb),
        grid=(nb,), in_specs=in_specs, out_specs=out_specs, out_shape=out_shape,
        scratch_shapes=[pltpu.VMEM((n, D), jnp.bfloat16), pltpu.VMEM((n, D), jnp.float32)],
        compiler_params=pltpu.CompilerParams(
            dimension_semantics=("arbitrary",), vmem_limit_bytes=VMEM_LIMIT),
        name="sample_layer",
    )(x, p, st, *shared)


def _shift_state(st, xa):
    depth, _, n, _ = st.shape
    nb = HIST // KS
    return pl.pallas_call(
        functools.partial(_state_kernel, nb=nb),
        grid=(depth, nb),
        in_specs=[
            pl.BlockSpec((None, KS, n, D), lambda i, j: (i, j, 0, 0)),
            pl.BlockSpec((None, 1, n, D),
                         lambda i, j: (i, jnp.minimum(KS * (j + 1), HIST - 1), 0, 0)),
            pl.BlockSpec((None, n, D), lambda i, j: (i, 0, 0)),
        ],
        out_specs=pl.BlockSpec((None, KS, n, D), lambda i, j: (i, j, 0, 0)),
        out_shape=jax.ShapeDtypeStruct(st.shape, jnp.float32),
        compiler_params=pltpu.CompilerParams(
            dimension_semantics=("arbitrary", "arbitrary"), vmem_limit_bytes=VMEM_LIMIT),
        name="shift_state",
    )(st, st, xa)


def _pack_rows(w, layers):
    _, k, n = w.shape
    nb = min(n, PACK_N)
    return pl.pallas_call(
        _pack_kernel,
        grid=(layers, n // nb),
        in_specs=[pl.BlockSpec((None, k, nb), lambda i, j: (i, 0, j))],
        out_specs=pl.BlockSpec((None, k // 2, nb), lambda i, j: (i, 0, j)),
        out_shape=jax.ShapeDtypeStruct((layers, k // 2, n), jnp.uint32),
        compiler_params=pltpu.CompilerParams(
            dimension_semantics=("arbitrary", "arbitrary"), vmem_limit_bytes=VMEM_LIMIT),
        name="pack_weights",
    )(w)


def kernel(x_prompt, x_sample, state_conv, p_prompt, p_sample, norm_g, w_in, conv_w, conv_b, ln_a_g, ln_a_b, w_proj_a, ln_v_g, ln_v_b, w_spatial, b_spatial, w_proj_b, w_out, ple_norm_g, w_ple_gate, b_ple_gate, w_ple, final_g):
    depth = w_in.shape[0]
    n_s = x_sample.shape[0]
    assert x_sample.shape[1] == 1
    xp = x_prompt
    xs = x_sample.reshape(n_s, D)
    ps = p_sample.reshape(depth, n_s, PLE)
    final_rows = jnp.broadcast_to(final_g[None, :], (depth, D))
    pad_rows = jnp.zeros((depth, N_VEC - 9, D), jnp.float32)
    vecs = jnp.concatenate(
        [jnp.stack([norm_g, conv_b, ln_a_g, ln_a_b, ln_v_g, ln_v_b, ple_norm_g, b_ple_gate,
                    final_rows], axis=1), pad_rows], axis=1)
    cws = jnp.pad(conv_w, ((0, 0), (0, CARRY - KCONV), (0, 0)))
    bst = jnp.swapaxes(b_spatial, 1, 2)
    big_w = (w_in, w_proj_a, w_proj_b, w_out, w_ple_gate)
    assert len(big_w) == N_BIG_W
    packed = tuple(_pack_rows(w, 1)[0] for w in big_w)
    w_ple_packed = _pack_rows(w_ple, depth)
    st = jnp.swapaxes(state_conv, 1, 2)
    conv_p, xa_s, vrow_p, vrow_s = [], [], [], []
    for i in range(depth):
        final = i == depth - 1
        shared = (vecs, cws, w_spatial, bst) + packed + (w_ple_packed,)
        xp, cp, vp, *packed_next = _prompt_layer(i, xp, p_prompt, shared, () if final else big_w)
        xs, xa, vs = _sample_layer(i, xs, ps, st, shared, final)
        packed = tuple(packed_next)
        conv_p.append(cp)
        xa_s.append(xa)
        vrow_p.append(vp)
        vrow_s.append(vs)
    conv_s = jnp.swapaxes(_shift_state(st, jnp.stack(xa_s)), 1, 2)
    return (xp, xs.reshape(n_s, 1, D), jnp.stack(conv_p), conv_s,
            jnp.stack(vrow_p), jnp.stack(vrow_s).reshape(depth, n_s, 1, D))
```

```python
import functools
import itertools

import jax
import jax.numpy as jnp
from jax.experimental import pallas as pl
from jax.experimental.pallas import tpu as pltpu

D = 1024
KCONV = 31
HIST = KCONV - 1
CHUNK = 128
GROUPS = 8
HD = D // GROUPS
PLE = 256
EPS = 1e-6
C_GLU, C_ZA, C_U, C_V, C_ZB, C_GA, C_GB, C_END = 0, 2048, 3072, 4096, 5120, 6144, 7168, 8192

TM = 256
CARRY = 32
SUB = 8
LEAD = CARRY - HIST
LANE = 128
MXU_N = 256
RB = 32
COST_GATE, COST_CONV, COST_ACT = 256, 330, 224
KS = 5
KSS = 10
PACK_N = 2048
VMEM_LIMIT = 56 * 1024 * 1024

V_NORM_G, V_CONV_B, V_LNA_G, V_LNA_B, V_LNV_G, V_LNV_B, V_PLE_G, V_PLE_B, V_FINAL_G = range(9)
N_VEC = 16
N_BIG_W = 5


def _bf(x):
    return x.astype(jnp.bfloat16)


def _dot(a, b):
    return jnp.dot(a, b, preferred_element_type=jnp.float32)


def _w(ref, c0=None, c1=None):
    words = ref[...] if c0 is None else ref[:, c0:c1]
    return pltpu.bitcast(words, jnp.bfloat16)


def _rms(x, g):
    return x * jax.lax.rsqrt(jnp.mean(x * x, axis=-1, keepdims=True) + EPS) * g


def _ln(x, g, b):
    mu = jnp.mean(x, axis=-1, keepdims=True)
    xc = x - mu
    var = jnp.mean(xc * xc, axis=-1, keepdims=True)
    return xc * jax.lax.rsqrt(var + EPS) * g + b


def _sigmoid(x):
    return jax.nn.sigmoid(x)


def _silu(x):
    return x * jax.nn.sigmoid(x)


def _row(vec, i):
    return vec[i:i + 1]


def _zero_row_of(v):
    bits = jax.lax.bitcast_convert_type(v, jnp.uint32)
    zero = jax.lax.shift_right_logical(jax.lax.shift_right_logical(bits, jnp.uint32(16)),
                                       jnp.uint32(16))
    return jax.lax.bitcast_convert_type(zero, jnp.float32)[0:1, :]


def _conv_lane_tile(xx_ref, cw_ref, acc_ref, bias, l0, take_zero):
    lanes = slice(l0, l0 + LANE)
    w = [cw_ref[k:k + 1, lanes] for k in range(KCONV)]

    def partial(r, row0, rows):
        part = None
        for q in range((KCONV + LEAD + SUB - 1) // SUB):
            k = SUB * q + r - LEAD
            if 0 <= k < KCONV:
                term = xx_ref[row0 + SUB * q:row0 + SUB * q + rows, lanes] * w[k]
                part = term if part is None else part + term
        return part

    tails = {r: partial(r, 0, SUB) for r in range(1, SUB)}
    for r0 in range(0, TM, RB):
        acc = partial(0, r0, RB) + bias[:, lanes]
        zero = take_zero()
        if zero is not None:
            acc = acc + zero
        for r in range(1, SUB):
            new = partial(r, r0 + SUB, RB)
            window = jnp.concatenate([tails[r], new], axis=0)
            acc = acc + pltpu.roll(window, RB + SUB - r, axis=0)[0:RB]
            tails[r] = new[RB - SUB:]
        acc_ref[r0:r0 + RB, lanes] = acc
        yield
    xx_ref[0:CARRY, lanes] = xx_ref[TM:TM + CARRY, lanes]
    yield


def _branch_inputs(conv, za, v, vec, za_done=False):
    ya = _silu(_ln(conv, _row(vec, V_LNA_G), _row(vec, V_LNA_B))) * (za if za_done else _silu(za))
    return _bf(ya), _ln(v, _row(vec, V_LNV_G), _row(vec, V_LNV_B))


def _tail(x, z, ya_in, vn, s_of_vn, p, vec, w_pa, w_pb, w_out, w_pg, w_ple, final,
          u_done=False):
    ya = _dot(ya_in, _w(w_pa))
    ug = z(C_U, C_V) if u_done else z(C_U, C_V) * _silu(z(C_ZB, C_GA))
    yb = _dot(_bf(ug * s_of_vn(vn)), _w(w_pb))
    m = _sigmoid(z(C_GA, C_GB)) * ya + _sigmoid(z(C_GB, C_END)) * yb
    x = x + _dot(_bf(m), _w(w_out))
    gate = _dot(_bf(_rms(x, _row(vec, V_PLE_G))), _w(w_pg)) + _row(vec, V_PLE_B)
    x = x + _sigmoid(gate) * _dot(_bf(p), _w(w_ple))
    if final:
        x = _rms(x, _row(vec, V_FINAL_G))
    return x


def _layer_kernel(x_ref, p_ref, xs_ref, ps_ref, st_ref, vec_ref, cw_ref, ws_ref, bs_ref, w_in,
                  w_pa, w_pb, w_out, w_pg, w_ple, *rest, final, n_seq, nb):
    n_next = 0 if final else N_BIG_W
    next_f32 = rest[:n_next]
    xo_ref, conv_ref, vrow_ref, xso_ref, xa_ref, vrs_ref = rest[n_next:n_next + 6]
    next_packed = rest[n_next + 6:2 * n_next + 6]
    xx_ref, z_ref, acc_ref, h_ref, hs_ref, accs_ref = rest[2 * n_next + 6:]
    shared = (vec_ref, cw_ref, ws_ref, bs_ref, w_in, w_pa, w_pb, w_out, w_pg, w_ple)
    i = pl.program_id(0)

    @pl.when(i < n_seq)
    def _():
        _prompt_step(x_ref, p_ref, *shared, next_f32, xo_ref, conv_ref, vrow_ref, next_packed,
                     xx_ref, z_ref, acc_ref, h_ref, final=final)

    @pl.when(i == n_seq)
    def _():
        _sample_step(xs_ref, ps_ref, st_ref, *shared, xso_ref, xa_ref, vrs_ref, hs_ref, accs_ref,
                     final=final, nb=nb)


def _prompt_step(x_ref, p_ref, vec_ref, cw_ref, ws_ref, bs_ref, w_in, w_pa, w_pb, w_out, w_pg,
                 w_ple, next_f32, xo_ref, conv_ref, vrow_ref, next_packed, xx_ref, z_ref, acc_ref,
                 h_ref, *, final):
    j = pl.program_id(1)

    @pl.when(j == 0)
    def _():
        xx_ref[0:CARRY, :] = jnp.zeros((CARRY, D), jnp.float32)

    vec = vec_ref[...]
    h_ref[...] = _bf(_rms(x_ref[...], _row(vec, V_NORM_G)))

    bias = _row(vec, V_CONV_B)
    popped = []
    in_flight = []

    def proj_piece(c0):
        res = _dot(h_ref[...], _w(w_in, c0, c0 + MXU_N))
        z_ref[:, c0:c0 + MXU_N] = res
        popped.extend(in_flight)
        in_flight[:] = [_zero_row_of(res[0:SUB, 0:LANE])]

    def take_zero():
        zero = None
        while popped:
            row = popped.pop()
            zero = row if zero is None else zero + row
        return zero

    def interleave(pieces, blocks, total_cost, needed):
        issued = spent = n = 0
        while True:
            while issued < len(pieces) and (issued < needed(n)
                                            or issued * total_cost <= spent * len(pieces)):
                proj_piece(pieces[issued])
                issued += 1
            cost = next(blocks, None)
            if cost is None:
                break
            spent += cost
            n += 1
        assert spent == total_cost and issued == len(pieces)

    def lane_tile_work(l0):
        lanes = slice(l0, l0 + LANE)
        gate = z_ref[:, D + l0:D + l0 + LANE]
        xx_ref[CARRY:CARRY + TM, lanes] = z_ref[:, lanes] * _sigmoid(gate)
        yield COST_GATE
        for _ in _conv_lane_tile(xx_ref, cw_ref, acc_ref, bias, l0, take_zero):
            yield COST_CONV

    def act_work():
        for r0 in range(0, TM, RB):
            rows = slice(r0, r0 + RB)
            z_ref[rows, C_ZA:C_U] = _silu(z_ref[rows, C_ZA:C_U])
            z_ref[rows, C_U:C_V] = z_ref[rows, C_U:C_V] * _silu(z_ref[rows, C_ZB:C_GA])
            yield COST_ACT

    glu_cols = [c for c0 in range(0, D, MXU_N) for c in (c0, D + c0)]
    tiles = list(range(0, D, LANE))
    blocks_per_tile = TM // RB + 2
    n_conv = len(tiles) * blocks_per_tile

    def needed(n):
        if n < n_conv:
            return 2 * (tiles[n // blocks_per_tile] // MXU_N) + 2
        return len(glu_cols) + (C_GA - C_ZA) // MXU_N

    interleave(glu_cols + list(range(C_ZA, C_END, MXU_N)),
               itertools.chain(*[lane_tile_work(l0) for l0 in tiles], act_work()),
               len(tiles) * (COST_GATE + (blocks_per_tile - 1) * COST_CONV)
               + (TM // RB) * COST_ACT, needed)

    conv_ref[...] = xx_ref[CARRY + TM - HIST:CARRY + TM, :]

    row = jax.lax.broadcasted_iota(jnp.int32, (CHUNK, CHUNK), 0)
    col = jax.lax.broadcasted_iota(jnp.int32, (CHUNK, CHUNK), 1)
    causal = row >= col
    bs = bs_ref[...]
    n_ch = TM // CHUNK

    def s_of_vn(vn):
        vb = _bf(vn)
        cols = []
        for g in range(GROUPS):
            wm = _bf(jnp.where(causal, ws_ref[g], 0.0))
            rhs = jnp.concatenate([vb[c * CHUNK:(c + 1) * CHUNK, g * HD:(g + 1) * HD]
                                   for c in range(n_ch)], axis=1)
            res = _dot(wm, rhs) + bs[:, g:g + 1]
            cols.append(jnp.concatenate([res[:, c * HD:(c + 1) * HD] for c in range(n_ch)],
                                        axis=0))
        return jnp.concatenate(cols, axis=1)

    def z(c0, c1):
        return z_ref[:, c0:c1]

    @pl.when(j >= 0)
    def _():
        for src, dst in zip(next_f32, next_packed):
            dst[...] = pltpu.bitcast(_bf(src[...]), jnp.uint32)
        ya_in, vn = _branch_inputs(acc_ref[...], z(C_ZA, C_U), z(C_V, C_ZB), vec, za_done=True)
        vrow_ref[...] = vn[TM - CHUNK:, :]
        xo_ref[...] = _tail(x_ref[...], z, ya_in, vn, s_of_vn, p_ref[...], vec, w_pa, w_pb,
                            w_out, w_pg, w_ple, final, u_done=True)


def _sample_step(x_ref, p_ref, st_ref, vec_ref, cw_ref, ws_ref, bs_ref, w_in, w_pa, w_pb, w_out,
                 w_pg, w_ple, xo_ref, xa_ref, vrow_ref, h_ref, acc_ref, *, final, nb):
    j = pl.program_id(1)
    vec = vec_ref[...]

    @pl.when(j == 0)
    def _():
        h = _bf(_rms(x_ref[...], _row(vec, V_NORM_G)))
        h_ref[...] = h
        glu = _dot(h, _w(w_in, C_GLU, C_ZA))
        xa = glu[:, :D] * _sigmoid(glu[:, D:])
        xa_ref[...] = xa
        acc_ref[...] = xa * cw_ref[HIST:KCONV, :] + _row(vec, V_CONV_B)

    @pl.when(j < nb)
    def _():
        part = acc_ref[...]
        for kk in range(KS):
            part = part + st_ref[kk] * cw_ref[pl.ds(j * KS + kk, 1), :]
        acc_ref[...] = part

    @pl.when(j == nb - 1)
    def _():
        lane_g = jax.lax.broadcasted_iota(jnp.int32, (1, D), 1) // HD
        w00 = jnp.zeros((1, D), jnp.float32)
        b0 = jnp.zeros((1, D), jnp.float32)
        for g in range(GROUPS):
            w00 = jnp.where(lane_g == g, ws_ref[g, 0:1, 0:1], w00)
            b0 = jnp.where(lane_g == g, bs_ref[0:1, g:g + 1], b0)

        def s_of_vn(vn):
            return _bf(w00).astype(jnp.float32) * _bf(vn).astype(jnp.float32) + b0

        h = h_ref[...]

        def z(c0, c1):
            return _dot(h, _w(w_in, c0, c1))

        ya_in, vn = _branch_inputs(acc_ref[...], z(C_ZA, C_U), z(C_V, C_ZB), vec)
        xo_ref[...] = _tail(x_ref[...], z, ya_in, vn, s_of_vn, p_ref[...], vec, w_pa, w_pb,
                            w_out, w_pg, w_ple, final)
        vrow_ref[...] = vn


def _state_kernel(st_ref, nxt_ref, xa_ref, o_ref, *, nb):
    j = pl.program_id(1)
    o_ref[0:KSS - 1] = st_ref[1:KSS]

    @pl.when(j < nb - 1)
    def _():
        o_ref[KSS - 1] = nxt_ref[0]

    @pl.when(j == nb - 1)
    def _():
        o_ref[KSS - 1] = xa_ref[...]


def _pack_kernel(w_ref, o_ref):
    o_ref[...] = pltpu.bitcast(_bf(w_ref[...]), jnp.uint32)


def _layer_block(layer, shape):
    nd = len(shape)
    return pl.BlockSpec((None,) + tuple(shape), lambda *_: (layer,) + (0,) * nd,
                        pipeline_mode=pl.Buffered(1))


def _resident(shape):
    nd = len(shape)
    return pl.BlockSpec(tuple(shape), lambda *_: (0,) * nd, pipeline_mode=pl.Buffered(1))


def _shared_specs(layer):
    return [
        _layer_block(layer, (N_VEC, D)), _layer_block(layer, (CARRY, D)),
        _layer_block(layer, (GROUPS, CHUNK, CHUNK)), _layer_block(layer, (CHUNK, GROUPS)),
        _resident((D // 2, C_END)), _resident((D // 2, D)), _resident((D // 2, D)),
        _resident((D // 2, D)), _resident((D // 2, D)), _layer_block(layer, (PLE // 2, D)),
    ]


def _layer(layer, x, p, xs, ps, st, shared, next_f32):
    b, t, _ = x.shape
    n = xs.shape[0]
    assert t % TM == 0 and TM % CHUNK == 0 and t >= HIST
    n_j = t // TM
    steps = b * n_j
    nb = HIST // KS
    assert nb * KS == HIST and nb <= n_j
    last = n_j - 1

    def tile(i, j):
        return jnp.minimum(i, b - 1), jnp.where(i < b, j, last)

    def step(i, j):
        return jnp.minimum(i * n_j + j, steps - 1)

    in_specs = [
        pl.BlockSpec((None, TM, D), lambda i, j: (*tile(i, j), 0)),
        pl.BlockSpec((None, None, TM, PLE), lambda i, j: (layer, *tile(i, j), 0)),
        _resident((n, D)), _layer_block(layer, (n, PLE)),
        pl.BlockSpec((None, KS, n, D),
                     lambda i, j: (layer, jnp.where(i < b, 0, jnp.minimum(j, nb - 1)), 0, 0)),
    ] + _shared_specs(layer)
    out_specs = [
        pl.BlockSpec((None, TM, D), lambda i, j: (*tile(i, j), 0)),
        pl.BlockSpec((None, HIST, D), lambda i, j: (jnp.minimum(i, b - 1), 0, 0)),
        pl.BlockSpec((None, CHUNK, D), lambda i, j: (jnp.minimum(i, b - 1), 0, 0)),
    ] + [pl.BlockSpec((n, D), lambda i, j: (0, 0))] * 3
    out_shape = [
        jax.ShapeDtypeStruct((b, t, D), jnp.float32),
        jax.ShapeDtypeStruct((b, HIST, D), jnp.float32),
        jax.ShapeDtypeStruct((b, CHUNK, D), jnp.float32),
    ] + [jax.ShapeDtypeStruct((n, D), jnp.float32)] * 3
    for w in next_f32:
        _, k, cols = w.shape
        rows = k // steps
        assert rows * steps == k and rows % (2 * SUB) == 0
        in_specs.append(pl.BlockSpec((None, rows, cols), lambda i, j: (layer + 1, step(i, j), 0)))
        out_specs.append(pl.BlockSpec((rows // 2, cols), lambda i, j: (step(i, j), 0)))
        out_shape.append(jax.ShapeDtypeStruct((k // 2, cols), jnp.uint32))
    return pl.pallas_call(
        functools.partial(_layer_kernel, final=not next_f32, n_seq=b, nb=nb),
        grid=(b + 1, n_j), in_specs=in_specs, out_specs=out_specs, out_shape=out_shape,
        scratch_shapes=[pltpu.VMEM((CARRY + TM, D), jnp.float32),
                        pltpu.VMEM((TM, C_END), jnp.float32),
                        pltpu.VMEM((TM, D), jnp.float32),
                        pltpu.VMEM((TM, D), jnp.bfloat16),
                        pltpu.VMEM((n, D), jnp.bfloat16),
                        pltpu.VMEM((n, D), jnp.float32)],
        compiler_params=pltpu.CompilerParams(
            dimension_semantics=("arbitrary", "arbitrary"), vmem_limit_bytes=VMEM_LIMIT),
        name="trunk_layer",
    )(x, p, xs, ps, st, *shared, *next_f32)


def _shift_state(st, xa):
    depth, _, n, _ = st.shape
    nb = HIST // KSS
    return pl.pallas_call(
        functools.partial(_state_kernel, nb=nb),
        grid=(depth, nb),
        in_specs=[
            pl.BlockSpec((None, KSS, n, D), lambda i, j: (i, j, 0, 0)),
            pl.BlockSpec((None, 1, n, D),
                         lambda i, j: (i, jnp.minimum(KSS * (j + 1), HIST - 1), 0, 0)),
            pl.BlockSpec((None, n, D), lambda i, j: (i, 0, 0)),
        ],
        out_specs=pl.BlockSpec((None, KSS, n, D), lambda i, j: (i, j, 0, 0)),
        out_shape=jax.ShapeDtypeStruct(st.shape, jnp.float32),
        compiler_params=pltpu.CompilerParams(
            dimension_semantics=("arbitrary", "arbitrary"), vmem_limit_bytes=VMEM_LIMIT),
        name="shift_state",
    )(st, st, xa)


def _pack_rows(w, layers):
    _, k, n = w.shape
    nb = min(n, PACK_N)
    return pl.pallas_call(
        _pack_kernel,
        grid=(layers, n // nb),
        in_specs=[pl.BlockSpec((None, k, nb), lambda i, j: (i, 0, j))],
        out_specs=pl.BlockSpec((None, k // 2, nb), lambda i, j: (i, 0, j)),
        out_shape=jax.ShapeDtypeStruct((layers, k // 2, n), jnp.uint32),
        compiler_params=pltpu.CompilerParams(
            dimension_semantics=("arbitrary", "arbitrary"), vmem_limit_bytes=VMEM_LIMIT),
        name="pack_weights",
    )(w)


def kernel(x_prompt, x_sample, state_conv, p_prompt, p_sample, norm_g, w_in, conv_w, conv_b, ln_a_g, ln_a_b, w_proj_a, ln_v_g, ln_v_b, w_spatial, b_spatial, w_proj_b, w_out, ple_norm_g, w_ple_gate, b_ple_gate, w_ple, final_g):
    depth = w_in.shape[0]
    n_s = x_sample.shape[0]
    assert x_sample.shape[1] == 1
    xp = x_prompt
    xs = x_sample.reshape(n_s, D)
    ps = p_sample.reshape(depth, n_s, PLE)
    final_rows = jnp.broadcast_to(final_g[None, :], (depth, D))
    pad_rows = jnp.zeros((depth, N_VEC - 9, D), jnp.float32)
    vecs = jnp.concatenate(
        [jnp.stack([norm_g, conv_b, ln_a_g, ln_a_b, ln_v_g, ln_v_b, ple_norm_g, b_ple_gate,
                    final_rows], axis=1), pad_rows], axis=1)
    cws = jnp.pad(conv_w, ((0, 0), (0, CARRY - KCONV), (0, 0)))
    bst = jnp.swapaxes(b_spatial, 1, 2)
    big_w = (w_in, w_proj_a, w_proj_b, w_out, w_ple_gate)
    assert len(big_w) == N_BIG_W
    packed = tuple(_pack_rows(w, 1)[0] for w in big_w)
    w_ple_packed = _pack_rows(w_ple, depth)
    st = jnp.swapaxes(state_conv, 1, 2)
    conv_p, xa_s, vrow_p, vrow_s = [], [], [], []
    for i in range(depth):
        final = i == depth - 1
        shared = (vecs, cws, w_spatial, bst) + packed + (w_ple_packed,)
        xp, cp, vp, xs, xa, vs, *packed_next = _layer(i, xp, p_prompt, xs, ps, st, shared,
                                                     () if final else big_w)
        packed = tuple(packed_next)
        conv_p.append(cp)
        xa_s.append(xa)
        vrow_p.append(vp)
        vrow_s.append(vs)
    conv_s = jnp.swapaxes(_shift_state(st, jnp.stack(xa_s)), 1, 2)
    return (xp, xs.reshape(n_s, 1, D), jnp.stack(conv_p), conv_s,
            jnp.stack(vrow_p), jnp.stack(vrow_s).reshape(depth, n_s, 1, D))
```

```python
import functools
import itertools

import jax
import jax.numpy as jnp
from jax.experimental import pallas as pl
from jax.experimental.pallas import tpu as pltpu

D = 1024
KCONV = 31
HIST = KCONV - 1
CHUNK = 128
GROUPS = 8
HD = D // GROUPS
PLE = 256
EPS = 1e-6
C_GLU, C_ZA, C_U, C_V, C_ZB, C_GA, C_GB, C_END = 0, 2048, 3072, 4096, 5120, 6144, 7168, 8192

TM = 256
CARRY = 32
SUB = 8
LEAD = CARRY - HIST
LANE = 128
MXU_N = 256
RB = 32
COST_GATE, COST_CONV, COST_ACT = 256, 330, 224
KS = 5
KSS = 10
PACK_N = 2048
VMEM_LIMIT = 56 * 1024 * 1024

V_NORM_G, V_CONV_B, V_LNA_G, V_LNA_B, V_LNV_G, V_LNV_B, V_PLE_G, V_PLE_B, V_FINAL_G = range(9)
N_VEC = 16
N_BIG_W = 5


def _bf(x):
    return x.astype(jnp.bfloat16)


def _dot(a, b):
    return jnp.dot(a, b, preferred_element_type=jnp.float32)


def _w(ref, c0=None, c1=None):
    words = ref[...] if c0 is None else ref[:, c0:c1]
    return pltpu.bitcast(words, jnp.bfloat16)


def _rms(x, g):
    return x * jax.lax.rsqrt(jnp.mean(x * x, axis=-1, keepdims=True) + EPS) * g


def _ln(x, g, b):
    mu = jnp.mean(x, axis=-1, keepdims=True)
    xc = x - mu
    var = jnp.mean(xc * xc, axis=-1, keepdims=True)
    return xc * jax.lax.rsqrt(var + EPS) * g + b


def _sigmoid(x):
    return jax.nn.sigmoid(x)


def _silu(x):
    return x * jax.nn.sigmoid(x)


def _row(vec, i):
    return vec[i:i + 1]


def _zero_row_of(v):
    bits = jax.lax.bitcast_convert_type(v, jnp.uint32)
    zero = jax.lax.shift_right_logical(jax.lax.shift_right_logical(bits, jnp.uint32(16)),
                                       jnp.uint32(16))
    return jax.lax.bitcast_convert_type(zero, jnp.float32)[0:1, :]


def _conv_lane_tile(xx_ref, cw_ref, acc_ref, bias, l0, take_zero):
    lanes = slice(l0, l0 + LANE)
    w = [cw_ref[k:k + 1, lanes] for k in range(KCONV)]

    def partial(r, row0, rows):
        part = None
        for q in range((KCONV + LEAD + SUB - 1) // SUB):
            k = SUB * q + r - LEAD
            if 0 <= k < KCONV:
                term = xx_ref[row0 + SUB * q:row0 + SUB * q + rows, lanes] * w[k]
                part = term if part is None else part + term
        return part

    tails = {r: partial(r, 0, SUB) for r in range(1, SUB)}
    for r0 in range(0, TM, RB):
        acc = partial(0, r0, RB) + bias[:, lanes]
        zero = take_zero()
        if zero is not None:
            acc = acc + zero
        for r in range(1, SUB):
            new = partial(r, r0 + SUB, RB)
            window = jnp.concatenate([tails[r], new], axis=0)
            acc = acc + pltpu.roll(window, RB + SUB - r, axis=0)[0:RB]
            tails[r] = new[RB - SUB:]
        acc_ref[r0:r0 + RB, lanes] = acc
        yield
    xx_ref[0:CARRY, lanes] = xx_ref[TM:TM + CARRY, lanes]
    yield


def _branch_inputs(conv, za, v, vec, za_done=False):
    ya = _silu(_ln(conv, _row(vec, V_LNA_G), _row(vec, V_LNA_B))) * (za if za_done else _silu(za))
    return _bf(ya), _ln(v, _row(vec, V_LNV_G), _row(vec, V_LNV_B))


def _tail(x, z, ya_in, vn, s_of_vn, p, vec, w_pa, w_pb, w_out, w_pg, w_ple, final,
          u_done=False):
    ya = _dot(ya_in, _w(w_pa))
    ug = z(C_U, C_V) if u_done else z(C_U, C_V) * _silu(z(C_ZB, C_GA))
    yb = _dot(_bf(ug * s_of_vn(vn)), _w(w_pb))
    m = _sigmoid(z(C_GA, C_GB)) * ya + _sigmoid(z(C_GB, C_END)) * yb
    x = x + _dot(_bf(m), _w(w_out))
    gate = _dot(_bf(_rms(x, _row(vec, V_PLE_G))), _w(w_pg)) + _row(vec, V_PLE_B)
    x = x + _sigmoid(gate) * _dot(_bf(p), _w(w_ple))
    if final:
        x = _rms(x, _row(vec, V_FINAL_G))
    return x


def _layer_kernel(x_ref, p_ref, xs_ref, ps_ref, st_ref, vec_ref, cw_ref, ws_ref, bs_ref, w_in,
                  w_pa, w_pb, w_out, w_pg, w_ple, *rest, final, n_j, steps, nb, spb):
    n_next = 0 if final else N_BIG_W
    next_f32 = rest[:n_next]
    xo_ref, conv_ref, vrow_ref, xso_ref, xa_ref, vrs_ref = rest[n_next:n_next + 6]
    next_packed = rest[n_next + 6:2 * n_next + 6]
    xx_ref, z_ref, acc_ref, h_ref, accs_ref = rest[2 * n_next + 6:]
    shared = (vec_ref, cw_ref, ws_ref, bs_ref, w_in, w_pa, w_pb, w_out, w_pg, w_ple)
    s = pl.program_id(0)

    @pl.when(s == 0)
    def _():
        accs_ref[...] = jnp.zeros(accs_ref.shape, jnp.float32)

    @pl.when((s % spb == 0) & (s < nb * spb))
    def _():
        k0 = (s // spb) * KS
        part = accs_ref[...]
        for kk in range(KS):
            part = part + st_ref[kk] * cw_ref[pl.ds(k0 + kk, 1), :]
        accs_ref[...] = part

    @pl.when(s < steps)
    def _():
        _prompt_step(s % n_j, x_ref, p_ref, *shared, next_f32, xo_ref, conv_ref, vrow_ref,
                     next_packed, xx_ref, z_ref, acc_ref, h_ref, final=final)

    @pl.when(s == steps)
    def _():
        _sample_step(xs_ref, ps_ref, *shared, xso_ref, xa_ref, vrs_ref, accs_ref, final=final)


def _prompt_step(j, x_ref, p_ref, vec_ref, cw_ref, ws_ref, bs_ref, w_in, w_pa, w_pb, w_out, w_pg,
                 w_ple, next_f32, xo_ref, conv_ref, vrow_ref, next_packed, xx_ref, z_ref, acc_ref,
                 h_ref, *, final):
    @pl.when(j == 0)
    def _():
        xx_ref[0:CARRY, :] = jnp.zeros((CARRY, D), jnp.float32)

    vec = vec_ref[...]
    h_ref[...] = _bf(_rms(x_ref[...], _row(vec, V_NORM_G)))

    bias = _row(vec, V_CONV_B)
    popped = []
    in_flight = []

    def proj_piece(c0):
        res = _dot(h_ref[...], _w(w_in, c0, c0 + MXU_N))
        z_ref[:, c0:c0 + MXU_N] = res
        popped.extend(in_flight)
        in_flight[:] = [_zero_row_of(res[0:SUB, 0:LANE])]

    def take_zero():
        zero = None
        while popped:
            row = popped.pop()
            zero = row if zero is None else zero + row
        return zero

    def interleave(pieces, blocks, total_cost, needed):
        issued = spent = n = 0
        while True:
            while issued < len(pieces) and (issued < needed(n)
                                            or issued * total_cost <= spent * len(pieces)):
                proj_piece(pieces[issued])
                issued += 1
            cost = next(blocks, None)
            if cost is None:
                break
            spent += cost
            n += 1
        assert spent == total_cost and issued == len(pieces)

    def lane_tile_work(l0):
        lanes = slice(l0, l0 + LANE)
        gate = z_ref[:, D + l0:D + l0 + LANE]
        xx_ref[CARRY:CARRY + TM, lanes] = z_ref[:, lanes] * _sigmoid(gate)
        yield COST_GATE
        for _ in _conv_lane_tile(xx_ref, cw_ref, acc_ref, bias, l0, take_zero):
            yield COST_CONV

    def act_work():
        for r0 in range(0, TM, RB):
            rows = slice(r0, r0 + RB)
            z_ref[rows, C_ZA:C_U] = _silu(z_ref[rows, C_ZA:C_U])
            z_ref[rows, C_U:C_V] = z_ref[rows, C_U:C_V] * _silu(z_ref[rows, C_ZB:C_GA])
            yield COST_ACT

    glu_cols = [c for c0 in range(0, D, MXU_N) for c in (c0, D + c0)]
    tiles = list(range(0, D, LANE))
    blocks_per_tile = TM // RB + 2
    n_conv = len(tiles) * blocks_per_tile

    def needed(n):
        if n < n_conv:
            return 2 * (tiles[n // blocks_per_tile] // MXU_N) + 2
        return len(glu_cols) + (C_GA - C_ZA) // MXU_N

    interleave(glu_cols + list(range(C_ZA, C_END, MXU_N)),
               itertools.chain(*[lane_tile_work(l0) for l0 in tiles], act_work()),
               len(tiles) * (COST_GATE + (blocks_per_tile - 1) * COST_CONV)
               + (TM // RB) * COST_ACT, needed)

    conv_ref[...] = xx_ref[CARRY + TM - HIST:CARRY + TM, :]

    row = jax.lax.broadcasted_iota(jnp.int32, (CHUNK, CHUNK), 0)
    col = jax.lax.broadcasted_iota(jnp.int32, (CHUNK, CHUNK), 1)
    causal = row >= col
    bs = bs_ref[...]
    n_ch = TM // CHUNK

    def s_of_vn(vn):
        vb = _bf(vn)
        cols = []
        for g in range(GROUPS):
            wm = _bf(jnp.where(causal, ws_ref[g], 0.0))
            rhs = jnp.concatenate([vb[c * CHUNK:(c + 1) * CHUNK, g * HD:(g + 1) * HD]
                                   for c in range(n_ch)], axis=1)
            res = _dot(wm, rhs) + bs[:, g:g + 1]
            cols.append(jnp.concatenate([res[:, c * HD:(c + 1) * HD] for c in range(n_ch)],
                                        axis=0))
        return jnp.concatenate(cols, axis=1)

    def z(c0, c1):
        return z_ref[:, c0:c1]

    @pl.when(j >= 0)
    def _():
        for src, dst in zip(next_f32, next_packed):
            dst[...] = pltpu.bitcast(_bf(src[...]), jnp.uint32)
        ya_in, vn = _branch_inputs(acc_ref[...], z(C_ZA, C_U), z(C_V, C_ZB), vec, za_done=True)
        vrow_ref[...] = vn[TM - CHUNK:, :]
        xo_ref[...] = _tail(x_ref[...], z, ya_in, vn, s_of_vn, p_ref[...], vec, w_pa, w_pb,
                            w_out, w_pg, w_ple, final, u_done=True)


def _sample_step(x_ref, p_ref, vec_ref, cw_ref, ws_ref, bs_ref, w_in, w_pa, w_pb, w_out, w_pg,
                 w_ple, xo_ref, xa_ref, vrow_ref, acc_ref, *, final):
    vec = vec_ref[...]
    h = _bf(_rms(x_ref[...], _row(vec, V_NORM_G)))
    glu = _dot(h, _w(w_in, C_GLU, C_ZA))
    xa = glu[:, :D] * _sigmoid(glu[:, D:])
    xa_ref[...] = xa
    conv = acc_ref[...] + xa * cw_ref[HIST:KCONV, :] + _row(vec, V_CONV_B)

    lane_g = jax.lax.broadcasted_iota(jnp.int32, (1, D), 1) // HD
    w00 = jnp.zeros((1, D), jnp.float32)
    b0 = jnp.zeros((1, D), jnp.float32)
    for g in range(GROUPS):
        w00 = jnp.where(lane_g == g, ws_ref[g, 0:1, 0:1], w00)
        b0 = jnp.where(lane_g == g, bs_ref[0:1, g:g + 1], b0)

    def s_of_vn(vn):
        return _bf(w00).astype(jnp.float32) * _bf(vn).astype(jnp.float32) + b0

    def z(c0, c1):
        return _dot(h, _w(w_in, c0, c1))

    ya_in, vn = _branch_inputs(conv, z(C_ZA, C_U), z(C_V, C_ZB), vec)
    xo_ref[...] = _tail(x_ref[...], z, ya_in, vn, s_of_vn, p_ref[...], vec, w_pa, w_pb,
                        w_out, w_pg, w_ple, final)
    vrow_ref[...] = vn


def _state_kernel(st_ref, nxt_ref, xa_ref, o_ref, *, nb):
    j = pl.program_id(1)
    o_ref[0:KSS - 1] = st_ref[1:KSS]

    @pl.when(j < nb - 1)
    def _():
        o_ref[KSS - 1] = nxt_ref[0]

    @pl.when(j == nb - 1)
    def _():
        o_ref[KSS - 1] = xa_ref[...]


def _pack_kernel(w_ref, o_ref):
    o_ref[...] = pltpu.bitcast(_bf(w_ref[...]), jnp.uint32)


def _layer_block(layer, shape):
    nd = len(shape)
    return pl.BlockSpec((None,) + tuple(shape), lambda *_: (layer,) + (0,) * nd,
                        pipeline_mode=pl.Buffered(1))


def _resident(shape):
    nd = len(shape)
    return pl.BlockSpec(tuple(shape), lambda *_: (0,) * nd, pipeline_mode=pl.Buffered(1))


def _shared_specs(layer):
    return [
        _layer_block(layer, (N_VEC, D)), _layer_block(layer, (CARRY, D)),
        _layer_block(layer, (GROUPS, CHUNK, CHUNK)), _layer_block(layer, (CHUNK, GROUPS)),
        _resident((D // 2, C_END)), _resident((D // 2, D)), _resident((D // 2, D)),
        _resident((D // 2, D)), _resident((D // 2, D)), _layer_block(layer, (PLE // 2, D)),
    ]


def _layer(layer, x, p, xs, ps, st, shared, next_f32):
    b, t, _ = x.shape
    n = xs.shape[0]
    assert t % TM == 0 and TM % CHUNK == 0 and t >= HIST
    n_j = t // TM
    steps = b * n_j
    nb = HIST // KS
    spb = steps // nb
    assert nb * KS == HIST and spb >= 1

    def tile(s):
        s = jnp.minimum(s, steps - 1)
        return s // n_j, s % n_j

    in_specs = [
        pl.BlockSpec((None, TM, D), lambda s: (*tile(s), 0)),
        pl.BlockSpec((None, None, TM, PLE), lambda s: (layer, *tile(s), 0)),
        _resident((n, D)), _layer_block(layer, (n, PLE)),
        pl.BlockSpec((None, KS, n, D), lambda s: (layer, jnp.minimum(s // spb, nb - 1), 0, 0)),
    ] + _shared_specs(layer)
    out_specs = [
        pl.BlockSpec((None, TM, D), lambda s: (*tile(s), 0)),
        pl.BlockSpec((None, HIST, D), lambda s: (tile(s)[0], 0, 0)),
        pl.BlockSpec((None, CHUNK, D), lambda s: (tile(s)[0], 0, 0)),
    ] + [pl.BlockSpec((n, D), lambda s: (0, 0))] * 3
    out_shape = [
        jax.ShapeDtypeStruct((b, t, D), jnp.float32),
        jax.ShapeDtypeStruct((b, HIST, D), jnp.float32),
        jax.ShapeDtypeStruct((b, CHUNK, D), jnp.float32),
    ] + [jax.ShapeDtypeStruct((n, D), jnp.float32)] * 3
    for w in next_f32:
        _, k, cols = w.shape
        rows = k // steps
        assert rows * steps == k and rows % (2 * SUB) == 0
        in_specs.append(pl.BlockSpec((None, rows, cols),
                                     lambda s: (layer + 1, jnp.minimum(s, steps - 1), 0)))
        out_specs.append(pl.BlockSpec((rows // 2, cols), lambda s: (jnp.minimum(s, steps - 1), 0)))
        out_shape.append(jax.ShapeDtypeStruct((k // 2, cols), jnp.uint32))
    return pl.pallas_call(
        functools.partial(_layer_kernel, final=not next_f32, n_j=n_j, steps=steps, nb=nb, spb=spb),
        grid=(steps + 1,), in_specs=in_specs, out_specs=out_specs, out_shape=out_shape,
        scratch_shapes=[pltpu.VMEM((CARRY + TM, D), jnp.float32),
                        pltpu.VMEM((TM, C_END), jnp.float32),
                        pltpu.VMEM((TM, D), jnp.float32),
                        pltpu.VMEM((TM, D), jnp.bfloat16),
                        pltpu.VMEM((n, D), jnp.float32)],
        compiler_params=pltpu.CompilerParams(
            dimension_semantics=("arbitrary",), vmem_limit_bytes=VMEM_LIMIT),
        name="trunk_layer",
    )(x, p, xs, ps, st, *shared, *next_f32)


def _shift_state(st, xa):
    depth, _, n, _ = st.shape
    nb = HIST // KSS
    return pl.pallas_call(
        functools.partial(_state_kernel, nb=nb),
        grid=(depth, nb),
        in_specs=[
            pl.BlockSpec((None, KSS, n, D), lambda i, j: (i, j, 0, 0)),
            pl.BlockSpec((None, 1, n, D),
                         lambda i, j: (i, jnp.minimum(KSS * (j + 1), HIST - 1), 0, 0)),
            pl.BlockSpec((None, n, D), lambda i, j: (i, 0, 0)),
        ],
        out_specs=pl.BlockSpec((None, KSS, n, D), lambda i, j: (i, j, 0, 0)),
        out_shape=jax.ShapeDtypeStruct(st.shape, jnp.float32),
        compiler_params=pltpu.CompilerParams(
            dimension_semantics=("arbitrary", "arbitrary"), vmem_limit_bytes=VMEM_LIMIT),
        name="shift_state",
    )(st, st, xa)


def _pack_rows(w, layers):
    _, k, n = w.shape
    nb = min(n, PACK_N)
    return pl.pallas_call(
        _pack_kernel,
        grid=(layers, n // nb),
        in_specs=[pl.BlockSpec((None, k, nb), lambda i, j: (i, 0, j))],
        out_specs=pl.BlockSpec((None, k // 2, nb), lambda i, j: (i, 0, j)),
        out_shape=jax.ShapeDtypeStruct((layers, k // 2, n), jnp.uint32),
        compiler_params=pltpu.CompilerParams(
            dimension_semantics=("arbitrary", "arbitrary"), vmem_limit_bytes=VMEM_LIMIT),
        name="pack_weights",
    )(w)


def kernel(x_prompt, x_sample, state_conv, p_prompt, p_sample, norm_g, w_in, conv_w, conv_b, ln_a_g, ln_a_b, w_proj_a, ln_v_g, ln_v_b, w_spatial, b_spatial, w_proj_b, w_out, ple_norm_g, w_ple_gate, b_ple_gate, w_ple, final_g):
    depth = w_in.shape[0]
    n_s = x_sample.shape[0]
    assert x_sample.shape[1] == 1
    xp = x_prompt
    xs = x_sample.reshape(n_s, D)
    ps = p_sample.reshape(depth, n_s, PLE)
    final_rows = jnp.broadcast_to(final_g[None, :], (depth, D))
    pad_rows = jnp.zeros((depth, N_VEC - 9, D), jnp.float32)
    vecs = jnp.concatenate(
        [jnp.stack([norm_g, conv_b, ln_a_g, ln_a_b, ln_v_g, ln_v_b, ple_norm_g, b_ple_gate,
                    final_rows], axis=1), pad_rows], axis=1)
    cws = jnp.pad(conv_w, ((0, 0), (0, CARRY - KCONV), (0, 0)))
    bst = jnp.swapaxes(b_spatial, 1, 2)
    big_w = (w_in, w_proj_a, w_proj_b, w_out, w_ple_gate)
    assert len(big_w) == N_BIG_W
    packed = tuple(_pack_rows(w, 1)[0] for w in big_w)
    w_ple_packed = _pack_rows(w_ple, depth)
    st = jnp.swapaxes(state_conv, 1, 2)
    conv_p, xa_s, vrow_p, vrow_s = [], [], [], []
    for i in range(depth):
        final = i == depth - 1
        shared = (vecs, cws, w_spatial, bst) + packed + (w_ple_packed,)
        xp, cp, vp, xs, xa, vs, *packed_next = _layer(i, xp, p_prompt, xs, ps, st, shared,
                                                     () if final else big_w)
        packed = tuple(packed_next)
        conv_p.append(cp)
        xa_s.append(xa)
        vrow_p.append(vp)
        vrow_s.append(vs)
    conv_s = jnp.swapaxes(_shift_state(st, jnp.stack(xa_s)), 1, 2)
    return (xp, xs.reshape(n_s, 1, D), jnp.stack(conv_p), conv_s,
            jnp.stack(vrow_p), jnp.stack(vrow_s).reshape(depth, n_s, 1, D))
```

```python
import functools
import itertools

import jax
import jax.numpy as jnp
from jax.experimental import pallas as pl
from jax.experimental.pallas import tpu as pltpu

D = 1024
KCONV = 31
HIST = KCONV - 1
CHUNK = 128
GROUPS = 8
HD = D // GROUPS
PLE = 256
EPS = 1e-6
C_GLU, C_ZA, C_U, C_V, C_ZB, C_GA, C_GB, C_END = 0, 2048, 3072, 4096, 5120, 6144, 7168, 8192

TM = 256
CARRY = 32
SUB = 8
LEAD = CARRY - HIST
LANE = 128
MXU_N = 256
RB = 32
COST_GATE, COST_CONV, COST_ACT = 256, 330, 224
KS = 5
PACK_N = 2048
VMEM_LIMIT = 56 * 1024 * 1024

V_NORM_G, V_CONV_B, V_LNA_G, V_LNA_B, V_LNV_G, V_LNV_B, V_PLE_G, V_PLE_B, V_FINAL_G = range(9)
N_VEC = 16
N_BIG_W = 5


def _bf(x):
    return x.astype(jnp.bfloat16)


def _dot(a, b):
    return jnp.dot(a, b, preferred_element_type=jnp.float32)


def _w(ref, c0=None, c1=None):
    words = ref[...] if c0 is None else ref[:, c0:c1]
    return pltpu.bitcast(words, jnp.bfloat16)


def _rms(x, g):
    return x * jax.lax.rsqrt(jnp.mean(x * x, axis=-1, keepdims=True) + EPS) * g


def _ln(x, g, b):
    mu = jnp.mean(x, axis=-1, keepdims=True)
    xc = x - mu
    var = jnp.mean(xc * xc, axis=-1, keepdims=True)
    return xc * jax.lax.rsqrt(var + EPS) * g + b


def _sigmoid(x):
    return jax.nn.sigmoid(x)


def _silu(x):
    return x * jax.nn.sigmoid(x)


def _row(vec, i):
    return vec[i:i + 1]


def _zero_row_of(v):
    bits = jax.lax.bitcast_convert_type(v, jnp.uint32)
    zero = jax.lax.shift_right_logical(jax.lax.shift_right_logical(bits, jnp.uint32(16)),
                                       jnp.uint32(16))
    return jax.lax.bitcast_convert_type(zero, jnp.float32)[0:1, :]


def _conv_lane_tile(xx_ref, cw_ref, acc_ref, bias, l0, take_zero):
    lanes = slice(l0, l0 + LANE)
    w = [cw_ref[k:k + 1, lanes] for k in range(KCONV)]

    def partial(r, row0, rows):
        part = None
        for q in range((KCONV + LEAD + SUB - 1) // SUB):
            k = SUB * q + r - LEAD
            if 0 <= k < KCONV:
                term = xx_ref[row0 + SUB * q:row0 + SUB * q + rows, lanes] * w[k]
                part = term if part is None else part + term
        return part

    tails = {r: partial(r, 0, SUB) for r in range(1, SUB)}
    for r0 in range(0, TM, RB):
        acc = partial(0, r0, RB) + bias[:, lanes]
        zero = take_zero()
        if zero is not None:
            acc = acc + zero
        for r in range(1, SUB):
            new = partial(r, r0 + SUB, RB)
            window = jnp.concatenate([tails[r], new], axis=0)
            acc = acc + pltpu.roll(window, RB + SUB - r, axis=0)[0:RB]
            tails[r] = new[RB - SUB:]
        acc_ref[r0:r0 + RB, lanes] = acc
        yield
    xx_ref[0:CARRY, lanes] = xx_ref[TM:TM + CARRY, lanes]
    yield


def _branch_inputs(conv, za, v, vec, za_done=False):
    ya = _silu(_ln(conv, _row(vec, V_LNA_G), _row(vec, V_LNA_B))) * (za if za_done else _silu(za))
    return _bf(ya), _ln(v, _row(vec, V_LNV_G), _row(vec, V_LNV_B))


def _tail(x, z, ya_in, vn, s_of_vn, p, vec, w_pa, w_pb, w_out, w_pg, w_ple, final,
          u_done=False):
    ya = _dot(ya_in, _w(w_pa))
    ug = z(C_U, C_V) if u_done else z(C_U, C_V) * _silu(z(C_ZB, C_GA))
    yb = _dot(_bf(ug * s_of_vn(vn)), _w(w_pb))
    m = _sigmoid(z(C_GA, C_GB)) * ya + _sigmoid(z(C_GB, C_END)) * yb
    x = x + _dot(_bf(m), _w(w_out))
    gate = _dot(_bf(_rms(x, _row(vec, V_PLE_G))), _w(w_pg)) + _row(vec, V_PLE_B)
    x = x + _sigmoid(gate) * _dot(_bf(p), _w(w_ple))
    if final:
        x = _rms(x, _row(vec, V_FINAL_G))
    return x


def _layer_kernel(x_ref, p_ref, xs_ref, ps_ref, st_ref, vec_ref, cw_ref, ws_ref, bs_ref, w_in,
                  w_pa, w_pb, w_out, w_pg, w_ple, *rest, final, n_j, steps, nb, spb):
    n_next = 0 if final else N_BIG_W
    next_f32 = rest[:n_next]
    xo_ref, conv_ref, vrow_ref, xso_ref, xa_ref, vrs_ref = rest[n_next:n_next + 6]
    next_packed = rest[n_next + 6:2 * n_next + 6]
    xx_ref, z_ref, acc_ref, h_ref, accs_ref = rest[2 * n_next + 6:]
    shared = (vec_ref, cw_ref, ws_ref, bs_ref, w_in, w_pa, w_pb, w_out, w_pg, w_ple)
    s = pl.program_id(0)

    @pl.when(s == 0)
    def _():
        accs_ref[...] = jnp.zeros(accs_ref.shape, jnp.float32)

    @pl.when((s % spb == 0) & (s < nb * spb))
    def _():
        k0 = (s // spb) * KS
        part = accs_ref[...]
        for kk in range(KS):
            part = part + st_ref[kk] * cw_ref[pl.ds(k0 + kk, 1), :]
        accs_ref[...] = part

    @pl.when(s < steps)
    def _():
        _prompt_step(s % n_j, x_ref, p_ref, *shared, next_f32, xo_ref, conv_ref, vrow_ref,
                     next_packed, xx_ref, z_ref, acc_ref, h_ref, final=final)

    @pl.when(s == steps)
    def _():
        _sample_step(xs_ref, ps_ref, *shared, xso_ref, xa_ref, vrs_ref, accs_ref, final=final)


def _prompt_step(j, x_ref, p_ref, vec_ref, cw_ref, ws_ref, bs_ref, w_in, w_pa, w_pb, w_out, w_pg,
                 w_ple, next_f32, xo_ref, conv_ref, vrow_ref, next_packed, xx_ref, z_ref, acc_ref,
                 h_ref, *, final):
    @pl.when(j == 0)
    def _():
        xx_ref[0:CARRY, :] = jnp.zeros((CARRY, D), jnp.float32)

    vec = vec_ref[...]
    h_ref[...] = _bf(_rms(x_ref[...], _row(vec, V_NORM_G)))

    bias = _row(vec, V_CONV_B)
    popped = []
    in_flight = []

    def proj_piece(c0):
        res = _dot(h_ref[...], _w(w_in, c0, c0 + MXU_N))
        z_ref[:, c0:c0 + MXU_N] = res
        popped.extend(in_flight)
        in_flight[:] = [_zero_row_of(res[0:SUB, 0:LANE])]

    def take_zero():
        zero = None
        while popped:
            row = popped.pop()
            zero = row if zero is None else zero + row
        return zero

    def interleave(pieces, blocks, total_cost, needed):
        issued = spent = n = 0
        while True:
            while issued < len(pieces) and (issued < needed(n)
                                            or issued * total_cost <= spent * len(pieces)):
                proj_piece(pieces[issued])
                issued += 1
            cost = next(blocks, None)
            if cost is None:
                break
            spent += cost
            n += 1
        assert spent == total_cost and issued == len(pieces)

    def lane_tile_work(l0):
        lanes = slice(l0, l0 + LANE)
        gate = z_ref[:, D + l0:D + l0 + LANE]
        xx_ref[CARRY:CARRY + TM, lanes] = z_ref[:, lanes] * _sigmoid(gate)
        yield COST_GATE
        for _ in _conv_lane_tile(xx_ref, cw_ref, acc_ref, bias, l0, take_zero):
            yield COST_CONV

    def act_work():
        for r0 in range(0, TM, RB):
            rows = slice(r0, r0 + RB)
            z_ref[rows, C_ZA:C_U] = _silu(z_ref[rows, C_ZA:C_U])
            z_ref[rows, C_U:C_V] = z_ref[rows, C_U:C_V] * _silu(z_ref[rows, C_ZB:C_GA])
            yield COST_ACT

    glu_cols = [c for c0 in range(0, D, MXU_N) for c in (c0, D + c0)]
    tiles = list(range(0, D, LANE))
    blocks_per_tile = TM // RB + 2
    n_conv = len(tiles) * blocks_per_tile

    def needed(n):
        if n < n_conv:
            return 2 * (tiles[n // blocks_per_tile] // MXU_N) + 2
        return len(glu_cols) + (C_GA - C_ZA) // MXU_N

    interleave(glu_cols + list(range(C_ZA, C_END, MXU_N)),
               itertools.chain(*[lane_tile_work(l0) for l0 in tiles], act_work()),
               len(tiles) * (COST_GATE + (blocks_per_tile - 1) * COST_CONV)
               + (TM // RB) * COST_ACT, needed)

    conv_ref[...] = xx_ref[CARRY + TM - HIST:CARRY + TM, :]

    row = jax.lax.broadcasted_iota(jnp.int32, (CHUNK, CHUNK), 0)
    col = jax.lax.broadcasted_iota(jnp.int32, (CHUNK, CHUNK), 1)
    causal = row >= col
    bs = bs_ref[...]
    n_ch = TM // CHUNK

    def s_of_vn(vn):
        vb = _bf(vn)
        cols = []
        for g in range(GROUPS):
            wm = _bf(jnp.where(causal, ws_ref[g], 0.0))
            rhs = jnp.concatenate([vb[c * CHUNK:(c + 1) * CHUNK, g * HD:(g + 1) * HD]
                                   for c in range(n_ch)], axis=1)
            res = _dot(wm, rhs) + bs[:, g:g + 1]
            cols.append(jnp.concatenate([res[:, c * HD:(c + 1) * HD] for c in range(n_ch)],
                                        axis=0))
        return jnp.concatenate(cols, axis=1)

    def z(c0, c1):
        return z_ref[:, c0:c1]

    @pl.when(j >= 0)
    def _():
        for src, dst in zip(next_f32, next_packed):
            dst[...] = pltpu.bitcast(_bf(src[...]), jnp.uint32)
        ya_in, vn = _branch_inputs(acc_ref[...], z(C_ZA, C_U), z(C_V, C_ZB), vec, za_done=True)
        vrow_ref[...] = vn[TM - CHUNK:, :]
        xo_ref[...] = _tail(x_ref[...], z, ya_in, vn, s_of_vn, p_ref[...], vec, w_pa, w_pb,
                            w_out, w_pg, w_ple, final, u_done=True)


def _sample_step(x_ref, p_ref, vec_ref, cw_ref, ws_ref, bs_ref, w_in, w_pa, w_pb, w_out, w_pg,
                 w_ple, xo_ref, xa_ref, vrow_ref, acc_ref, *, final):
    vec = vec_ref[...]
    h = _bf(_rms(x_ref[...], _row(vec, V_NORM_G)))
    glu = _dot(h, _w(w_in, C_GLU, C_ZA))
    xa = glu[:, :D] * _sigmoid(glu[:, D:])
    xa_ref[...] = xa
    conv = acc_ref[...] + xa * cw_ref[HIST:KCONV, :] + _row(vec, V_CONV_B)

    lane_g = jax.lax.broadcasted_iota(jnp.int32, (1, D), 1) // HD
    w00 = jnp.zeros((1, D), jnp.float32)
    b0 = jnp.zeros((1, D), jnp.float32)
    for g in range(GROUPS):
        w00 = jnp.where(lane_g == g, ws_ref[g, 0:1, 0:1], w00)
        b0 = jnp.where(lane_g == g, bs_ref[0:1, g:g + 1], b0)

    def s_of_vn(vn):
        return _bf(w00).astype(jnp.float32) * _bf(vn).astype(jnp.float32) + b0

    def z(c0, c1):
        return _dot(h, _w(w_in, c0, c1))

    ya_in, vn = _branch_inputs(conv, z(C_ZA, C_U), z(C_V, C_ZB), vec)
    xo_ref[...] = _tail(x_ref[...], z, ya_in, vn, s_of_vn, p_ref[...], vec, w_pa, w_pb,
                        w_out, w_pg, w_ple, final)
    vrow_ref[...] = vn


def _state_kernel(st_ref, xa_ref, o_ref, sems):
    copies = []
    for i in range(st_ref.shape[0]):
        copies.append(pltpu.make_async_copy(
            st_ref.at[i, pl.ds(1, HIST - 1)], o_ref.at[i, pl.ds(0, HIST - 1)], sems.at[2 * i]))
        copies.append(pltpu.make_async_copy(xa_ref.at[i], o_ref.at[i, HIST - 1],
                                            sems.at[2 * i + 1]))
    for copy in copies:
        copy.start()
    for copy in copies:
        copy.wait()


def _pack_kernel(w_ref, o_ref):
    o_ref[...] = pltpu.bitcast(_bf(w_ref[...]), jnp.uint32)


def _layer_block(layer, shape):
    nd = len(shape)
    return pl.BlockSpec((None,) + tuple(shape), lambda *_: (layer,) + (0,) * nd,
                        pipeline_mode=pl.Buffered(1))


def _resident(shape):
    nd = len(shape)
    return pl.BlockSpec(tuple(shape), lambda *_: (0,) * nd, pipeline_mode=pl.Buffered(1))


def _shared_specs(layer):
    return [
        _layer_block(layer, (N_VEC, D)), _layer_block(layer, (CARRY, D)),
        _layer_block(layer, (GROUPS, CHUNK, CHUNK)), _layer_block(layer, (CHUNK, GROUPS)),
        _resident((D // 2, C_END)), _resident((D // 2, D)), _resident((D // 2, D)),
        _resident((D // 2, D)), _resident((D // 2, D)), _layer_block(layer, (PLE // 2, D)),
    ]


def _layer(layer, x, p, xs, ps, st, shared, next_f32):
    b, t, _ = x.shape
    n = xs.shape[0]
    assert t % TM == 0 and TM % CHUNK == 0 and t >= HIST
    n_j = t // TM
    steps = b * n_j
    nb = HIST // KS
    spb = steps // nb
    assert nb * KS == HIST and spb >= 1

    def tile(s):
        s = jnp.minimum(s, steps - 1)
        return s // n_j, s % n_j

    in_specs = [
        pl.BlockSpec((None, TM, D), lambda s: (*tile(s), 0)),
        pl.BlockSpec((None, None, TM, PLE), lambda s: (layer, *tile(s), 0)),
        _resident((n, D)), _layer_block(layer, (n, PLE)),
        pl.BlockSpec((None, KS, n, D), lambda s: (layer, jnp.minimum(s // spb, nb - 1), 0, 0)),
    ] + _shared_specs(layer)
    out_specs = [
        pl.BlockSpec((None, TM, D), lambda s: (*tile(s), 0)),
        pl.BlockSpec((None, HIST, D), lambda s: (tile(s)[0], 0, 0)),
        pl.BlockSpec((None, CHUNK, D), lambda s: (tile(s)[0], 0, 0)),
    ] + [pl.BlockSpec((n, D), lambda s: (0, 0))] * 3
    out_shape = [
        jax.ShapeDtypeStruct((b, t, D), jnp.float32),
        jax.ShapeDtypeStruct((b, HIST, D), jnp.float32),
        jax.ShapeDtypeStruct((b, CHUNK, D), jnp.float32),
    ] + [jax.ShapeDtypeStruct((n, D), jnp.float32)] * 3
    for w in next_f32:
        _, k, cols = w.shape
        rows = k // steps
        assert rows * steps == k and rows % (2 * SUB) == 0
        in_specs.append(pl.BlockSpec((None, rows, cols),
                                     lambda s: (layer + 1, jnp.minimum(s, steps - 1), 0)))
        out_specs.append(pl.BlockSpec((rows // 2, cols), lambda s: (jnp.minimum(s, steps - 1), 0)))
        out_shape.append(jax.ShapeDtypeStruct((k // 2, cols), jnp.uint32))
    return pl.pallas_call(
        functools.partial(_layer_kernel, final=not next_f32, n_j=n_j, steps=steps, nb=nb, spb=spb),
        grid=(steps + 1,), in_specs=in_specs, out_specs=out_specs, out_shape=out_shape,
        scratch_shapes=[pltpu.VMEM((CARRY + TM, D), jnp.float32),
                        pltpu.VMEM((TM, C_END), jnp.float32),
                        pltpu.VMEM((TM, D), jnp.float32),
                        pltpu.VMEM((TM, D), jnp.bfloat16),
                        pltpu.VMEM((n, D), jnp.float32)],
        compiler_params=pltpu.CompilerParams(
            dimension_semantics=("arbitrary",), vmem_limit_bytes=VMEM_LIMIT),
        name="trunk_layer",
    )(x, p, xs, ps, st, *shared, *next_f32)


def _shift_state(st, xa):
    depth = st.shape[0]
    return pl.pallas_call(
        _state_kernel,
        in_specs=[pl.BlockSpec(memory_space=pl.ANY), pl.BlockSpec(memory_space=pl.ANY)],
        out_specs=pl.BlockSpec(memory_space=pl.ANY),
        out_shape=jax.ShapeDtypeStruct(st.shape, jnp.float32),
        scratch_shapes=[pltpu.SemaphoreType.DMA((2 * depth,))],
        name="shift_state",
    )(st, xa)


def _pack_rows(w, layers):
    _, k, n = w.shape
    nb = min(n, PACK_N)
    return pl.pallas_call(
        _pack_kernel,
        grid=(layers, n // nb),
        in_specs=[pl.BlockSpec((None, k, nb), lambda i, j: (i, 0, j))],
        out_specs=pl.BlockSpec((None, k // 2, nb), lambda i, j: (i, 0, j)),
        out_shape=jax.ShapeDtypeStruct((layers, k // 2, n), jnp.uint32),
        compiler_params=pltpu.CompilerParams(
            dimension_semantics=("arbitrary", "arbitrary"), vmem_limit_bytes=VMEM_LIMIT),
        name="pack_weights",
    )(w)


def kernel(x_prompt, x_sample, state_conv, p_prompt, p_sample, norm_g, w_in, conv_w, conv_b, ln_a_g, ln_a_b, w_proj_a, ln_v_g, ln_v_b, w_spatial, b_spatial, w_proj_b, w_out, ple_norm_g, w_ple_gate, b_ple_gate, w_ple, final_g):
    depth = w_in.shape[0]
    n_s = x_sample.shape[0]
    assert x_sample.shape[1] == 1
    xp = x_prompt
    xs = x_sample.reshape(n_s, D)
    ps = p_sample.reshape(depth, n_s, PLE)
    final_rows = jnp.broadcast_to(final_g[None, :], (depth, D))
    pad_rows = jnp.zeros((depth, N_VEC - 9, D), jnp.float32)
    vecs = jnp.concatenate(
        [jnp.stack([norm_g, conv_b, ln_a_g, ln_a_b, ln_v_g, ln_v_b, ple_norm_g, b_ple_gate,
                    final_rows], axis=1), pad_rows], axis=1)
    cws = jnp.pad(conv_w, ((0, 0), (0, CARRY - KCONV), (0, 0)))
    bst = jnp.swapaxes(b_spatial, 1, 2)
    big_w = (w_in, w_proj_a, w_proj_b, w_out, w_ple_gate)
    assert len(big_w) == N_BIG_W
    packed = tuple(_pack_rows(w, 1)[0] for w in big_w)
    w_ple_packed = _pack_rows(w_ple, depth)
    st = jnp.swapaxes(state_conv, 1, 2)
    conv_p, xa_s, vrow_p, vrow_s = [], [], [], []
    for i in range(depth):
        final = i == depth - 1
        shared = (vecs, cws, w_spatial, bst) + packed + (w_ple_packed,)
        xp, cp, vp, xs, xa, vs, *packed_next = _layer(i, xp, p_prompt, xs, ps, st, shared,
                                                     () if final else big_w)
        packed = tuple(packed_next)
        conv_p.append(cp)
        xa_s.append(xa)
        vrow_p.append(vp)
        vrow_s.append(vs)
    conv_s = jnp.swapaxes(_shift_state(st, jnp.stack(xa_s)), 1, 2)
    return (xp, xs.reshape(n_s, 1, D), jnp.stack(conv_p), conv_s,
            jnp.stack(vrow_p), jnp.stack(vrow_s).reshape(depth, n_s, 1, D))
```
